```python
import jax, jax.numpy as jnp
from jax import lax
import numpy as np

D_MODEL = 1024
BATCH = 4
SEQ = 8192
DEPTH = 4
DEC_BATCH = 32
DEC_SEQ = 64
PAST_LEN = 1024

CHUNK = 64
N_MIXERS = 3
N_HEADS = 8
HEAD_DIM = D_MODEL // N_HEADS
D_ATT = N_HEADS * HEAD_DIM
IDX_HEADS = 4
IDX_DIM = 64
TOPK_MAX = 256
Q_BLOCK = 64
ROPE_THETA = 500000.0
D_CONV = D_MODEL
CONV_WIDTH = 31
D_POOL = D_MODEL
POOL_WINDOWS = (2, 4, 8, 16)
N_POOL_GROUPS = 4
POOL_GROUP = D_POOL // N_POOL_GROUPS
POOL_HIST = max(POOL_WINDOWS) - 1
LN_EPS = 1e-5
DEEPNORM_ALPHA = (2.0 * DEPTH) ** 0.25
DEEPNORM_BETA = (8.0 * DEPTH) ** -0.25
N_A = (DEPTH + 2) // 3
N_B = (DEPTH + 1) // 3
N_C = DEPTH // 3
A_SPLITS = (D_ATT, 2 * D_ATT, 3 * D_ATT, 4 * D_ATT,
            4 * D_ATT + IDX_HEADS * IDX_DIM,
            4 * D_ATT + IDX_HEADS * IDX_DIM + IDX_DIM)
A_IN = A_SPLITS[-1] + IDX_HEADS

kernel_name = 'hybrid_dsa_conv_pool_stream_step'

F32 = jnp.float32


def _layer_norm(x, g, b):
    xf = x.astype(F32)
    mu = jnp.mean(xf, axis=-1, keepdims=True)
    var = jnp.mean(jnp.square(xf - mu), axis=-1, keepdims=True)
    return ((xf - mu) * lax.rsqrt(var + LN_EPS) * g.astype(F32) + b.astype(F32)).astype(x.dtype)


def _rope(x, pos):
    r = x.shape[-1] // 4
    half = r // 2
    inv = ROPE_THETA ** (-jnp.arange(half, dtype=F32) * 2.0 / r)
    ang = pos.astype(F32)[:, None] * inv[None, :]
    cos = jnp.cos(ang)[None, :, None, :]
    sin = jnp.sin(ang)[None, :, None, :]
    xf = x.astype(F32)
    x1 = xf[..., :half]
    x2 = xf[..., half:r]
    out = jnp.concatenate([x1 * cos - x2 * sin, x2 * cos + x1 * sin, xf[..., r:]], axis=-1)
    return out.astype(x.dtype)


def _attn_project(x, w_in, pos):
    B, T, _ = x.shape
    h = x @ w_in
    q, k, v, g, qi, ki, wi = jnp.split(h, A_SPLITS, axis=-1)
    q = _rope(q.reshape(B, T, N_HEADS, HEAD_DIM), pos)
    k = _rope(k.reshape(B, T, N_HEADS, HEAD_DIM), pos)
    v = v.reshape(B, T, N_HEADS, HEAD_DIM)
    qi = _rope(qi.reshape(B, T, IDX_HEADS, IDX_DIM), pos)
    ki = _rope(ki[:, :, None, :], pos)[:, :, 0, :]
    wi = wi * (IDX_HEADS ** -0.5)
    return q, k, v, g, qi, ki, wi


def _sparse_attend(q, qi, wi, qpos, K, V, KI, kpos, topk):
    dots = jnp.einsum('bthd,bsd->bths', qi.astype(F32), KI.astype(F32)) * (IDX_DIM ** -0.5)
    iscore = jnp.einsum('bth,bths->bts', wi.astype(F32), jax.nn.relu(dots))
    adm = (kpos[None, :] // CHUNK) <= (qpos[:, None] // CHUNK)
    iscore = jnp.where(adm[None], iscore, -jnp.inf)
    _, idx = lax.top_k(iscore, topk)
    valid = (kpos[idx] // CHUNK) <= (qpos[None, :, None] // CHUNK)
    gather = jax.vmap(lambda kb, ib: kb[ib])
    Kg = gather(K, idx)
    Vg = gather(V, idx)
    s = jnp.einsum('bthd,btjhd->bthj', q.astype(F32), Kg.astype(F32)) * (HEAD_DIM ** -0.5)
    s = jnp.where(valid[:, :, None, :], s, -jnp.inf)
    p = jax.nn.softmax(s, axis=-1)
    o = jnp.einsum('bthj,btjhd->bthd', p, Vg.astype(F32))
    return o.astype(q.dtype)


def _attn_mixer_prompt(x, w_in, w_out):
    B, T, _ = x.shape
    pos = jnp.arange(T)
    q, k, v, g, qi, ki, wi = _attn_project(x, w_in, pos)
    topk = min(TOPK_MAX, T // 4)
    nb = T // Q_BLOCK

    def blk(a):
        return jnp.swapaxes(a.reshape(B, nb, Q_BLOCK, *a.shape[2:]), 0, 1)

    ob = lax.map(lambda a: _sparse_attend(a[0], a[1], a[2], a[3], k, v, ki, pos, topk),
                 (blk(q), blk(qi), blk(wi), pos.reshape(nb, Q_BLOCK)))
    o = jnp.swapaxes(ob, 0, 1).reshape(B, T, D_ATT)
    y = (o * jax.nn.silu(g)) @ w_out
    return y, k, v, ki


def _attn_mixer_sample(x, ck, cv, cki, w_in, w_out):
    B, T, _ = x.shape
    past = ck.shape[1]
    pos = past + jnp.arange(T)
    q, k, v, g, qi, ki, wi = _attn_project(x, w_in, pos)
    K = jnp.concatenate([ck.astype(k.dtype), k], axis=1)
    V = jnp.concatenate([cv.astype(v.dtype), v], axis=1)
    KI = jnp.concatenate([cki.astype(ki.dtype), ki], axis=1)
    kpos = jnp.arange(past + T)
    topk = min(TOPK_MAX, (past + T) // 4)
    o = _sparse_attend(q, qi, wi, pos, K, V, KI, kpos, topk).reshape(B, T, D_ATT)
    y = (o * jax.nn.silu(g)) @ w_out
    return y, k, v, ki


def _conv_mixer(x, prev, w_in, conv_w, conv_b, n_g, n_b, w_out):
    h = x @ w_in
    a, b, g = jnp.split(h, 3, axis=-1)
    u = a * jax.nn.sigmoid(b)
    ext = jnp.concatenate([prev.astype(u.dtype), u], axis=1)
    c = lax.conv_general_dilated(ext, conv_w[:, None, :].astype(u.dtype), window_strides=(1,),
                                 padding='VALID', dimension_numbers=('NWC', 'WIO', 'NWC'),
                                 feature_group_count=D_CONV) + conv_b
    c = jax.nn.silu(_layer_norm(c, n_g, n_b))
    y = (c * jax.nn.silu(g)) @ w_out
    return y, ext[:, -(CONV_WIDTH - 1):]


def _pool_mixer(x, prev, pos, w_in, w_grp, scale, w_out):
    B, T, _ = x.shape
    h = x @ w_in
    u, g = jnp.split(h, 2, axis=-1)
    ext = jnp.concatenate([prev.astype(u.dtype), u], axis=1)
    P = POOL_HIST
    cs = jnp.cumsum(jnp.concatenate([jnp.zeros_like(ext[:, :1]), ext], axis=1).astype(F32), axis=1)
    hi = cs[:, P + 1:P + 1 + T]
    uf = u.astype(F32)
    outs = []
    for gi, w in enumerate(POOL_WINDOWS):
        sl = slice(gi * POOL_GROUP, (gi + 1) * POOL_GROUP)
        lo = cs[:, P + 1 - w:P + 1 - w + T, sl]
        cnt = jnp.minimum(pos + 1, w).astype(F32)[None, :, None]
        outs.append((hi[..., sl] - lo) / cnt - uf[..., sl])
    d = jnp.stack(outs, axis=2)
    mixed = jnp.einsum('btgc,gcd->btgd', d, w_grp.astype(F32)).reshape(B, T, D_POOL) * scale.astype(F32)
    y = (mixed.astype(x.dtype) * jax.nn.silu(g)) @ w_out
    return y, ext[:, -P:]


def setup_inputs(seed: int = 0) -> dict:
    key = jax.random.key(seed)
    ks = jax.random.split(key, 24)

    def nrm(k, shape, s):
        return jax.random.normal(k, shape, F32) * s

    return {
        'x_prompt': nrm(ks[0], (BATCH, SEQ, D_MODEL), 1.0),
        'x_sample': nrm(ks[1], (DEC_BATCH, DEC_SEQ, D_MODEL), 1.0),
        'cache_k': nrm(ks[2], (N_A, DEC_BATCH, PAST_LEN, N_HEADS, HEAD_DIM), 1.0),
        'cache_v': nrm(ks[3], (N_A, DEC_BATCH, PAST_LEN, N_HEADS, HEAD_DIM), 1.0),
        'cache_kidx': nrm(ks[4], (N_A, DEC_BATCH, PAST_LEN, IDX_DIM), 1.0),
        'state_conv': nrm(ks[5], (N_B, DEC_BATCH, CONV_WIDTH - 1, D_CONV), 0.5),
        'state_pool': nrm(ks[6], (N_C, DEC_BATCH, POOL_HIST, D_POOL), 1.0),
        'w_in_a': nrm(ks[7], (N_A, D_MODEL, A_IN), D_MODEL ** -0.5),
        'w_out_a': nrm(ks[8], (N_A, D_ATT, D_MODEL), D_ATT ** -0.5 * DEEPNORM_BETA),
        'w_in_b': nrm(ks[9], (N_B, D_MODEL, 3 * D_CONV), D_MODEL ** -0.5),
        'conv_w_b': nrm(ks[10], (N_B, CONV_WIDTH, D_CONV), CONV_WIDTH ** -0.5),
        'conv_bias_b': nrm(ks[11], (N_B, D_CONV), 0.02),
        'norm_g_b': 1.0 + nrm(ks[12], (N_B, D_CONV), 0.02),
        'norm_b_b': nrm(ks[13], (N_B, D_CONV), 0.02),
        'w_out_b': nrm(ks[14], (N_B, D_CONV, D_MODEL), D_CONV ** -0.5 * DEEPNORM_BETA),
        'w_in_c': nrm(ks[15], (N_C, D_MODEL, 2 * D_POOL), D_MODEL ** -0.5),
        'w_grp_c': nrm(ks[16], (N_C, N_POOL_GROUPS, POOL_GROUP, POOL_GROUP), POOL_GROUP ** -0.5),
        'scale_c': 1.0 + nrm(ks[17], (N_C, D_POOL), 0.02),
        'w_out_c': nrm(ks[18], (N_C, D_POOL, D_MODEL), D_POOL ** -0.5 * DEEPNORM_BETA),
        'ln_g': 1.0 + nrm(ks[19], (DEPTH, D_MODEL), 0.02),
        'ln_b': nrm(ks[20], (DEPTH, D_MODEL), 0.02),
    }


def reference(x_prompt, x_sample, cache_k, cache_v, cache_kidx, state_conv, state_pool,
              w_in_a, w_out_a, w_in_b, conv_w_b, conv_bias_b, norm_g_b, norm_b_b, w_out_b,
              w_in_c, w_grp_c, scale_c, w_out_c, ln_g, ln_b):
    xp, xs = x_prompt, x_sample
    Tp = xp.shape[1]
    Ts = xs.shape[1]
    past = cache_k.shape[2]
    kp, vp, kip, cvp, plp = [], [], [], [], []
    ksm, vsm, kism, cvs, pls = [], [], [], [], []
    for i in range(DEPTH):
        m = i % N_MIXERS
        j = i // N_MIXERS
        if m == 0:
            yp, k1, v1, ki1 = _attn_mixer_prompt(xp, w_in_a[j], w_out_a[j])
            ys, k2, v2, ki2 = _attn_mixer_sample(xs, cache_k[j], cache_v[j], cache_kidx[j],
                                                 w_in_a[j], w_out_a[j])
            kp.append(k1); vp.append(v1); kip.append(ki1)
            ksm.append(k2); vsm.append(v2); kism.append(ki2)
        elif m == 1:
            zeros = jnp.zeros((xp.shape[0], CONV_WIDTH - 1, D_CONV), xp.dtype)
            yp, c1 = _conv_mixer(xp, zeros, w_in_b[j], conv_w_b[j], conv_bias_b[j],
                                 norm_g_b[j], norm_b_b[j], w_out_b[j])
            ys, c2 = _conv_mixer(xs, state_conv[j], w_in_b[j], conv_w_b[j], conv_bias_b[j],
                                 norm_g_b[j], norm_b_b[j], w_out_b[j])
            cvp.append(c1); cvs.append(c2)
        else:
            zeros = jnp.zeros((xp.shape[0], POOL_HIST, D_POOL), xp.dtype)
            yp, p1 = _pool_mixer(xp, zeros, jnp.arange(Tp), w_in_c[j], w_grp_c[j], scale_c[j], w_out_c[j])
            ys, p2 = _pool_mixer(xs, state_pool[j], past + jnp.arange(Ts), w_in_c[j], w_grp_c[j],
                                 scale_c[j], w_out_c[j])
            plp.append(p1); pls.append(p2)
        xp = _layer_norm(DEEPNORM_ALPHA * xp + yp, ln_g[i], ln_b[i])
        xs = _layer_norm(DEEPNORM_ALPHA * xs + ys, ln_g[i], ln_b[i])
    return (xp, xs,
            jnp.stack(kp), jnp.stack(vp), jnp.stack(kip), jnp.stack(cvp), jnp.stack(plp),
            jnp.stack(ksm), jnp.stack(vsm), jnp.stack(kism), jnp.stack(cvs), jnp.stack(pls))
```

```python
import functools

import jax
import jax.numpy as jnp
import numpy as np
from jax import lax
from jax.experimental import pallas as pl
from jax.experimental.pallas import tpu as pltpu

F32, BF16, I32 = jnp.float32, jnp.bfloat16, jnp.int32

LANES = 128
CHUNK = 64
N_HEADS, HEAD_DIM = 8, 128
IDX_HEADS, IDX_DIM = 4, 64
TOPK_MAX = 256
ROPE_THETA = 500000.0
CONV_WIDTH = 31
CONV_HIST = 32
POOL_WINDOWS = (2, 4, 8, 16)
POOL_HIST = 15
POOL_PAD = 16
LN_EPS = 1e-5
NEG_BIG = -1e30
F32_MAX = float(np.finfo(np.float32).max)
F32_TINY = float(np.finfo(np.float32).tiny)
TINY_KEY = 0x00800000
INT_MIN = -(2 ** 31)
NINF_KEY = int(np.int32(np.uint32(0xFF800000) ^ np.uint32(0x7FFFFFFF)))
IDX_BIG = 2 ** 30
VMEM_LIMIT = 56 * 1024 * 1024


def _cparams(sem):
    return pltpu.CompilerParams(dimension_semantics=sem, vmem_limit_bytes=VMEM_LIMIT)


def _dot(a, b):
    return jnp.dot(a, b, preferred_element_type=F32)


def _dot_nt(a, b):
    return lax.dot_general(a, b, (((1,), (1,)), ((), ())), preferred_element_type=F32)


def _layer_norm(z, g, b):
    mu = jnp.mean(z, axis=-1, keepdims=True)
    zc = z - mu
    var = jnp.mean(zc * zc, axis=-1, keepdims=True)
    return zc * lax.rsqrt(var + LN_EPS) * g + b


def _silu(x):
    return x * jax.nn.sigmoid(x)


def _rope(x, c, sa, sb, half):
    return x * c + pltpu.roll(x, LANES - half, 1) * sa + pltpu.roll(x, half, 1) * sb


def _rope_tables(pos):
    def tab(dim):
        r = dim // 4
        half = r // 2
        inv = ROPE_THETA ** (-jnp.arange(half, dtype=F32) * 2.0 / r)
        ang = pos.astype(F32)[:, None] * inv[None, :]
        cos, sin = jnp.cos(ang), jnp.sin(ang)
        n = pos.shape[0]
        c = jnp.concatenate([cos, cos, jnp.ones((n, LANES - r), F32)], axis=1)
        sa = jnp.concatenate([-sin, jnp.zeros((n, LANES - half), F32)], axis=1)
        sb = jnp.concatenate([jnp.zeros((n, half), F32), sin, jnp.zeros((n, LANES - r), F32)], axis=1)
        return [c, sa, sb]
    return jnp.stack(tab(HEAD_DIM) + tab(IDX_DIM), axis=0)


IDX_SLABS = IDX_HEADS + 1


def _aproj_kernel(x_ref, rope_ref, wq_ref, wk_ref, wv_ref, wg_ref, wi_ref,
                  k32_ref, v32_ref, idx32_ref, q_ref, kb_ref, vb_ref, sg_ref, qis_ref, kib_ref):
    xb = x_ref[...].astype(BF16)
    cq, saq, sbq = rope_ref[0], rope_ref[1], rope_ref[2]
    ci, sai, sbi = rope_ref[3], rope_ref[4], rope_ref[5]

    hq = _dot(xb, wq_ref[...])
    for h in range(N_HEADS):
        sl = slice(h * HEAD_DIM, (h + 1) * HEAD_DIM)
        q_ref[:, sl] = (_rope(hq[:, sl], cq, saq, sbq, HEAD_DIM // 8) * (HEAD_DIM ** -0.5)).astype(BF16)
    hk = _dot(xb, wk_ref[...])
    for h in range(N_HEADS):
        sl = slice(h * HEAD_DIM, (h + 1) * HEAD_DIM)
        kh = _rope(hk[:, sl], cq, saq, sbq, HEAD_DIM // 8)
        k32_ref[:, sl] = kh
        kb_ref[:, sl] = kh.astype(BF16)
    hv = _dot(xb, wv_ref[...])
    v32_ref[...] = hv
    vb_ref[...] = hv.astype(BF16)
    sg_ref[...] = _silu(_dot(xb, wg_ref[...])).astype(BF16)

    hi = _dot(xb, wi_ref[...])
    for h in range(IDX_HEADS):
        sl = slice(h * LANES, (h + 1) * LANES)
        qis_ref[:, sl] = (_rope(hi[:, sl], ci, sai, sbi, IDX_DIM // 8) * (IDX_DIM ** -0.5)).astype(BF16)
    last = _rope(hi[:, IDX_HEADS * LANES:], ci, sai, sbi, IDX_DIM // 8)
    idx32_ref[...] = last
    lane = lax.broadcasted_iota(I32, last.shape, 1)
    kib_ref[...] = jnp.where(lane < IDX_DIM, last, 0.0).astype(BF16)


def _split_a_weights(w_in):
    d = w_in.shape[0]
    da = N_HEADS * HEAD_DIM
    wq, wk, wv, wg = (w_in[:, i * da:(i + 1) * da].astype(BF16) for i in range(4))
    o = 4 * da
    slabs = []
    for h in range(IDX_HEADS):
        slabs += [w_in[:, o + h * IDX_DIM:o + (h + 1) * IDX_DIM], jnp.zeros((d, LANES - IDX_DIM), w_in.dtype)]
    o += IDX_HEADS * IDX_DIM
    slabs += [w_in[:, o:o + IDX_DIM + IDX_HEADS], jnp.zeros((d, LANES - IDX_DIM - IDX_HEADS), w_in.dtype)]
    wi = jnp.concatenate(slabs, axis=1).astype(BF16)
    return wq, wk, wv, wg, wi


def _aproj(x2, rope, weights, tm):
    n, d = x2.shape
    da = N_HEADS * HEAD_DIM
    rope_tiles = rope.shape[1] // tm
    row = lambda i: (i, 0)
    const = lambda i: (0, 0)
    wq, wk, wv, wg, wi = weights
    out_shape = (
        jax.ShapeDtypeStruct((n, da), F32), jax.ShapeDtypeStruct((n, da), F32),
        jax.ShapeDtypeStruct((n, LANES), F32),
        jax.ShapeDtypeStruct((n, da), BF16), jax.ShapeDtypeStruct((n, da), BF16),
        jax.ShapeDtypeStruct((n, da), BF16), jax.ShapeDtypeStruct((n, da), BF16),
        jax.ShapeDtypeStruct((n, IDX_HEADS * LANES), BF16), jax.ShapeDtypeStruct((n, LANES), BF16),
    )
    return pl.pallas_call(
        _aproj_kernel,
        grid=(n // tm,),
        in_specs=[pl.BlockSpec((tm, d), row),
                  pl.BlockSpec((6, tm, LANES), lambda i: (0, i % rope_tiles, 0)),
                  pl.BlockSpec((d, da), const), pl.BlockSpec((d, da), const),
                  pl.BlockSpec((d, da), const), pl.BlockSpec((d, da), const),
                  pl.BlockSpec((d, IDX_SLABS * LANES), const)],
        out_specs=[pl.BlockSpec((tm, da), row), pl.BlockSpec((tm, da), row), pl.BlockSpec((tm, LANES), row),
                   pl.BlockSpec((tm, da), row), pl.BlockSpec((tm, da), row), pl.BlockSpec((tm, da), row),
                   pl.BlockSpec((tm, da), row), pl.BlockSpec((tm, IDX_HEADS * LANES), row),
                   pl.BlockSpec((tm, LANES), row)],
        out_shape=out_shape,
        compiler_params=_cparams(("parallel",)),
        name="a_proj",
    )(x2, rope, wq, wk, wv, wg, wi)


def _index_scores(qis_ref, idx32_ref, kib_ref, q_row0, k_col0, adm_base):
    tq = qis_ref.shape[0]
    tk = kib_ref.shape[0]
    kib = kib_ref[...]
    w = idx32_ref[:, IDX_DIM:IDX_DIM + IDX_HEADS] * (IDX_HEADS ** -0.5)
    s = None
    for h in range(IDX_HEADS):
        d = _dot_nt(qis_ref[:, h * LANES:(h + 1) * LANES], kib)
        t = jnp.maximum(d, 0.0) * w[:, h:h + 1]
        s = t if s is None else s + t
    s = jnp.where(jnp.abs(s) < F32_TINY, 0.0, s)
    kidx = k_col0 + lax.broadcasted_iota(I32, (tq, tk), 1)
    rows = q_row0 + lax.broadcasted_iota(I32, (tq, 1), 0)
    n_adm = adm_base + (rows // CHUNK + 1) * CHUNK
    return jnp.where(kidx < n_adm, s, -jnp.inf), kidx


def _to_key(s):
    bits = lax.bitcast_convert_type(s, I32)
    return bits ^ ((bits >> 31) & 0x7FFFFFFF)


def _next_key(key):
    return jnp.where(key == 0, TINY_KEY, jnp.where(key == -TINY_KEY - 1, 0, key + 1))


def _from_key(key):
    return lax.bitcast_convert_type(key ^ ((key >> 31) & 0x7FFFFFFF), F32)


def _pair_tables(n_q, tq, tk, adm_base):
    qt, kt = [], []
    for q in range(n_q):
        last = (adm_base + ((q * tq + tq - 1) // CHUNK + 1) * CHUNK - 1) // tk
        for k in range(last + 1):
            qt.append(q)
            kt.append(k)
    return jnp.asarray(qt, I32), jnp.asarray(kt, I32)


def _last_key_tile(qt, tq, tk, adm_base):
    return (adm_base + ((qt * tq + tq - 1) // CHUNK + 1) * CHUNK - 1) // tk


def _select_kernel(qt_ref, kt_ref, qis_ref, idx32_ref, kib_ref, thr_ref, thrn_ref, cut_ref, keys_ref,
                   *, tq, tk, adm_base, topk, idx_bits):
    p = pl.program_id(1)
    qt = qt_ref[p]
    kt = kt_ref[p]
    s, _ = _index_scores(qis_ref, idx32_ref, kib_ref, qt * tq, kt * tk, adm_base)
    keys_ref[kt] = _to_key(s)

    @pl.when(kt == _last_key_tile(qt, tq, tk, adm_base))
    def _search():
        n_tiles = kt + 1
        cols = tk // LANES

        def count(pred, cand):
            def body(t, acc):
                tile = keys_ref[t]
                for c in range(cols):
                    acc = acc + jnp.where(pred(tile[:, c * LANES:(c + 1) * LANES], cand), 1, 0)
                return acc
            acc = lax.fori_loop(0, n_tiles, body, jnp.zeros((tq, LANES), I32))
            total = jnp.sum(acc.astype(F32), axis=1, keepdims=True).astype(I32)
            return jnp.broadcast_to(total, (tq, LANES))

        ge = lambda v, c: v >= c
        lt = lambda v, c: v < c

        zero = jnp.zeros((tq, LANES), I32)
        base = jnp.where(count(ge, zero) >= topk, zero, INT_MIN)

        def value_bit(i, base):
            cand = base | jnp.left_shift(jnp.int32(1), 30 - i)
            return jnp.where(count(ge, cand) >= topk, cand, base)
        thr = lax.fori_loop(0, 31, value_bit, base)

        need = topk - count(ge, thr + 1)

        def mark(t, carry):
            kidx = t * tk + lax.broadcasted_iota(I32, (tq, tk), 1)
            keys_ref[t] = jnp.where(keys_ref[t] == jnp.tile(thr, (1, cols)), kidx, IDX_BIG)
            return carry
        lax.fori_loop(0, n_tiles, mark, 0)

        def index_bit(i, pos):
            cand = pos | jnp.left_shift(jnp.int32(1), idx_bits - 1 - i)
            return jnp.where(count(lt, cand) < need, cand, pos)
        cut = lax.fori_loop(0, idx_bits, index_bit, zero) + 1

        all_adm = thr <= NINF_KEY
        thr_ref[...] = jnp.where(all_adm, -F32_MAX, _from_key(thr))
        thrn_ref[...] = jnp.where(all_adm, -F32_MAX, _from_key(_next_key(thr)))
        cut_ref[...] = jnp.where(all_adm, IDX_BIG, cut)


def _select(qis, idx32, kib, *, tq, tk, adm_base, topk):
    b, t, _ = qis.shape
    s_pad = kib.shape[1]
    n_kt = s_pad // tk
    qt_tab, kt_tab = _pair_tables(t // tq, tq, tk, adm_base)
    qmap = lambda bi, p, qt, kt: (bi, qt[p], 0)
    kmap = lambda bi, p, qt, kt: (bi, kt[p], 0)
    kern = functools.partial(_select_kernel, tq=tq, tk=tk, adm_base=adm_base, topk=topk,
                             idx_bits=int(s_pad).bit_length())
    out = jax.ShapeDtypeStruct((b, t, LANES), F32)
    return pl.pallas_call(
        kern,
        grid_spec=pltpu.PrefetchScalarGridSpec(
            num_scalar_prefetch=2,
            grid=(b, int(qt_tab.shape[0])),
            in_specs=[pl.BlockSpec((None, tq, IDX_HEADS * LANES), qmap),
                      pl.BlockSpec((None, tq, LANES), qmap),
                      pl.BlockSpec((None, tk, LANES), kmap)],
            out_specs=[pl.BlockSpec((None, tq, LANES), qmap), pl.BlockSpec((None, tq, LANES), qmap),
                       pl.BlockSpec((None, tq, LANES), qmap)],
            scratch_shapes=[pltpu.VMEM((n_kt, tq, tk), I32)]),
        out_shape=(out, out, jax.ShapeDtypeStruct((b, t, LANES), I32)),
        compiler_params=_cparams(("parallel", "arbitrary")),
        name="a_select",
    )(qt_tab, kt_tab, qis, idx32, kib)


def _attn_kernel(qt_ref, kt_ref, q_ref, sg_ref, x_ref, k_ref, v_ref, qis_ref, idx32_ref, kib_ref,
                 thr_ref, thrn_ref, cut_ref, wo_ref, lg_ref, lb_ref, xn_ref, m_ref, l_ref, acc_ref,
                 *, tq, tk, adm_base, alpha):
    p = pl.program_id(1)
    qt = qt_ref[p]
    kt = kt_ref[p]
    reps = tk // LANES

    @pl.when(kt == 0)
    def _init():
        m_ref[...] = jnp.full(m_ref.shape, NEG_BIG, F32)
        l_ref[...] = jnp.zeros(l_ref.shape, F32)
        acc_ref[...] = jnp.zeros(acc_ref.shape, F32)

    s_idx, kidx = _index_scores(qis_ref, idx32_ref, kib_ref, qt * tq, kt * tk, adm_base)
    bound = jnp.where(kidx < jnp.tile(cut_ref[...], (1, reps)),
                      jnp.tile(thr_ref[...], (1, reps)), jnp.tile(thrn_ref[...], (1, reps)))
    sel = s_idx >= bound

    for h in range(N_HEADS):
        sl = slice(h * HEAD_DIM, (h + 1) * HEAD_DIM)
        s = jnp.where(sel, _dot_nt(q_ref[:, sl], k_ref[:, sl]), NEG_BIG)
        m_prev = m_ref[h]
        m_new = jnp.maximum(m_prev, jnp.max(s, axis=1, keepdims=True))
        a = jnp.exp(m_prev - m_new)
        pr = jnp.exp(s - jnp.tile(m_new, (1, reps)))
        l_ref[h] = a * l_ref[h] + jnp.sum(pr, axis=1, keepdims=True)
        acc_ref[:, sl] = a * acc_ref[:, sl] + _dot(pr.astype(BF16), v_ref[:, sl])
        m_ref[h] = m_new

    @pl.when(kt == _last_key_tile(qt, tq, tk, adm_base))
    def _finish():
        for h in range(N_HEADS):
            sl = slice(h * HEAD_DIM, (h + 1) * HEAD_DIM)
            acc_ref[:, sl] = acc_ref[:, sl] / l_ref[h]
        y = _dot((acc_ref[...] * sg_ref[...].astype(F32)).astype(BF16), wo_ref[...])
        xn_ref[...] = _layer_norm(alpha * x_ref[...] + y, lg_ref[...], lb_ref[...])


def _attend(q, sg, x, kb, vb, qis, idx32, kib, thr, thrn, cut, wo, lg, lb, *, tq, tk, adm_base, alpha):
    b, t, da = q.shape
    d = x.shape[-1]
    qt_tab, kt_tab = _pair_tables(t // tq, tq, tk, adm_base)
    qmap = lambda bi, p, qt, kt: (bi, qt[p], 0)
    kmap = lambda bi, p, qt, kt: (bi, kt[p], 0)
    const = lambda bi, p, qt, kt: (0, 0)
    kern = functools.partial(_attn_kernel, tq=tq, tk=tk, adm_base=adm_base, alpha=alpha)
    return pl.pallas_call(
        kern,
        grid_spec=pltpu.PrefetchScalarGridSpec(
            num_scalar_prefetch=2,
            grid=(b, int(qt_tab.shape[0])),
            in_specs=[pl.BlockSpec((None, tq, da), qmap), pl.BlockSpec((None, tq, da), qmap),
                      pl.BlockSpec((None, tq, d), qmap),
                      pl.BlockSpec((None, tk, da), kmap), pl.BlockSpec((None, tk, da), kmap),
                      pl.BlockSpec((None, tq, IDX_HEADS * LANES), qmap),
                      pl.BlockSpec((None, tq, LANES), qmap),
                      pl.BlockSpec((None, tk, LANES), kmap),
                      pl.BlockSpec((None, tq, LANES), qmap), pl.BlockSpec((None, tq, LANES), qmap),
                      pl.BlockSpec((None, tq, LANES), qmap),
                      pl.BlockSpec((da, d), const), pl.BlockSpec((1, d), const), pl.BlockSpec((1, d), const)],
            out_specs=pl.BlockSpec((None, tq, d), qmap),
            scratch_shapes=[pltpu.VMEM((N_HEADS, tq, LANES), F32), pltpu.VMEM((N_HEADS, tq, LANES), F32),
                            pltpu.VMEM((tq, da), F32)]),
        out_shape=jax.ShapeDtypeStruct((b, t, d), F32),
        compiler_params=_cparams(("parallel", "arbitrary")),
        name="a_attend",
    )(qt_tab, kt_tab, q, sg, x, kb, vb, qis, idx32, kib, thr, thrn, cut, wo, lg, lb)


def _attn_layer(x, rope, past, w_in, w_out, lg, lb, alpha, *, tm, tq, tk):
    b, t, d = x.shape
    da = N_HEADS * HEAD_DIM
    k32, v32, idx32, q, kb, vb, sg, qis, kib = _aproj(x.reshape(b * t, d), rope, _split_a_weights(w_in), tm)
    r3 = lambda a: a.reshape(b, t, a.shape[-1])
    k32, v32, idx32, q, kb, vb, sg, qis, kib = map(r3, (k32, v32, idx32, q, kb, vb, sg, qis, kib))
    if past is None:
        adm_base, kb_all, vb_all, kib_all = 0, kb, vb, kib
        total = t
    else:
        ck, cv, cki = past
        adm_base = ck.shape[1]
        total = adm_base + t
        pad = (-total) % LANES
        zpad = lambda w: jnp.zeros((b, pad, w), BF16)
        kb_all = jnp.concatenate([ck.astype(BF16), kb, zpad(da)], axis=1)
        vb_all = jnp.concatenate([cv.astype(BF16), vb, zpad(da)], axis=1)
        cki_pad = jnp.concatenate([cki.astype(BF16), jnp.zeros((b, adm_base, LANES - IDX_DIM), BF16)], axis=2)
        kib_all = jnp.concatenate([cki_pad, kib, zpad(LANES)], axis=1)
        tk = kb_all.shape[1]
    topk = min(TOPK_MAX, total // 4)
    thr, thrn, cut = _select(qis, idx32, kib_all, tq=tq, tk=tk, adm_base=adm_base, topk=topk)
    xn = _attend(q, sg, x, kb_all, vb_all, qis, idx32, kib_all, thr, thrn, cut,
                 w_out.astype(BF16), lg, lb, tq=tq, tk=tk, adm_base=adm_base, alpha=alpha)
    return xn, k32.reshape(b, t, N_HEADS, HEAD_DIM), v32.reshape(b, t, N_HEADS, HEAD_DIM), idx32[:, :, :IDX_DIM]


def _conv_kernel(x_ref, prev_ref, wa_ref, wb_ref, wg_ref, cw_ref, cb_ref, ng_ref, nb_ref, wo_ref,
                 lg_ref, lb_ref, xn_ref, st_ref, ext_ref, c_ref, *, tm, alpha):
    t = pl.program_id(1)

    @pl.when(t == 0)
    def _first():
        ext_ref[0:CONV_HIST] = prev_ref[...]

    @pl.when(t > 0)
    def _carry():
        ext_ref[0:CONV_HIST] = ext_ref[tm:tm + CONV_HIST]

    x = x_ref[...]
    xb = x.astype(BF16)
    ext_ref[CONV_HIST:CONV_HIST + tm] = _dot(xb, wa_ref[...]) * jax.nn.sigmoid(_dot(xb, wb_ref[...]))
    st_ref[...] = ext_ref[tm:tm + CONV_HIST]

    off = CONV_HIST - (CONV_WIDTH - 1)
    for cs in range(x.shape[1] // LANES):
        sl = slice(cs * LANES, (cs + 1) * LANES)
        acc = jnp.broadcast_to(cb_ref[:, sl], (tm, LANES))
        for w in range(CONV_WIDTH):
            acc = acc + ext_ref[off + w:off + w + tm, sl] * cw_ref[w:w + 1, sl]
        c_ref[:, sl] = acc

    cn = _silu(_layer_norm(c_ref[...], ng_ref[...], nb_ref[...]))
    y = _dot((cn * _silu(_dot(xb, wg_ref[...]))).astype(BF16), wo_ref[...])
    xn_ref[...] = _layer_norm(alpha * x + y, lg_ref[...], lb_ref[...])


def _conv_layer(x, prev, w_in, conv_w, conv_b, n_g, n_b, w_out, lg, lb, alpha, *, tm):
    b, t, d = x.shape
    dc = conv_w.shape[1]
    assert t >= CONV_HIST and t % tm == 0
    wa, wb, wg = (w_in[:, i * dc:(i + 1) * dc].astype(BF16) for i in range(3))
    prev = jnp.concatenate([jnp.zeros((b, CONV_HIST - (CONV_WIDTH - 1), dc), F32), prev], axis=1)
    cw = jnp.concatenate([conv_w, jnp.zeros((CONV_HIST - CONV_WIDTH, dc), F32)], axis=0)
    row = lambda bi, ti: (bi, ti, 0)
    per_b = lambda bi, ti: (bi, 0, 0)
    const = lambda bi, ti: (0, 0)
    vec = pl.BlockSpec((1, dc), const)
    xn, st = pl.pallas_call(
        functools.partial(_conv_kernel, tm=tm, alpha=alpha),
        grid=(b, t // tm),
        in_specs=[pl.BlockSpec((None, tm, d), row), pl.BlockSpec((None, CONV_HIST, dc), per_b),
                  pl.BlockSpec((d, dc), const), pl.BlockSpec((d, dc), const), pl.BlockSpec((d, dc), const),
                  pl.BlockSpec((CONV_HIST, dc), const), vec, vec, vec,
                  pl.BlockSpec((dc, d), const), pl.BlockSpec((1, d), const), pl.BlockSpec((1, d), const)],
        out_specs=[pl.BlockSpec((None, tm, d), row), pl.BlockSpec((None, CONV_HIST, dc), per_b)],
        out_shape=(jax.ShapeDtypeStruct((b, t, d), F32), jax.ShapeDtypeStruct((b, CONV_HIST, dc), F32)),
        scratch_shapes=[pltpu.VMEM((CONV_HIST + tm, dc), F32), pltpu.VMEM((tm, dc), F32)],
        compiler_params=_cparams(("parallel", "arbitrary")),
        name="b_conv",
    )(x, prev, wa, wb, wg, cw, conv_b.reshape(1, dc), n_g.reshape(1, dc), n_b.reshape(1, dc),
      w_out.astype(BF16), lg, lb)
    return xn, st[:, CONV_HIST - (CONV_WIDTH - 1):]


def _pool_kernel(x_ref, prev_ref, wu_ref, wg_ref, wgrp_ref, sc_ref, wo_ref, lg_ref, lb_ref,
                 xn_ref, st_ref, ext_ref, mix_ref, *, tm, pos0, alpha):
    t = pl.program_id(1)

    @pl.when(t == 0)
    def _first():
        ext_ref[0:POOL_PAD] = prev_ref[...]

    @pl.when(t > 0)
    def _carry():
        ext_ref[0:POOL_PAD] = ext_ref[tm:tm + POOL_PAD]

    x = x_ref[...]
    xb = x.astype(BF16)
    ext_ref[POOL_PAD:POOL_PAD + tm] = _dot(xb, wu_ref[...])
    st_ref[...] = ext_ref[tm:tm + POOL_PAD]

    pos = pos0 + t * tm + lax.broadcasted_iota(I32, (tm, 1), 0)
    grp = x.shape[1] // len(POOL_WINDOWS)
    for gi, w in enumerate(POOL_WINDOWS):
        sl = slice(gi * grp, (gi + 1) * grp)
        u = ext_ref[POOL_PAD:POOL_PAD + tm, sl]
        ws = u
        for i in range(1, w):
            ws = ws + ext_ref[POOL_PAD - i:POOL_PAD - i + tm, sl]
        cnt = jnp.minimum(pos + 1, w).astype(F32)
        dlt = ws / cnt - u
        mix_ref[:, sl] = _dot(dlt.astype(BF16), wgrp_ref[gi]) * sc_ref[:, sl]

    y = _dot((mix_ref[...] * _silu(_dot(xb, wg_ref[...]))).astype(BF16), wo_ref[...])
    xn_ref[...] = _layer_norm(alpha * x + y, lg_ref[...], lb_ref[...])


def _pool_layer(x, prev, pos0, w_in, w_grp, scale, w_out, lg, lb, alpha, *, tm):
    b, t, d = x.shape
    dp = scale.shape[0]
    grp = dp // len(POOL_WINDOWS)
    assert t >= POOL_PAD and t % tm == 0
    wu, wg = (w_in[:, i * dp:(i + 1) * dp].astype(BF16) for i in range(2))
    prev = jnp.concatenate([jnp.zeros((b, POOL_PAD - POOL_HIST, dp), F32), prev], axis=1)
    row = lambda bi, ti: (bi, ti, 0)
    per_b = lambda bi, ti: (bi, 0, 0)
    const = lambda bi, ti: (0, 0)
    xn, st = pl.pallas_call(
        functools.partial(_pool_kernel, tm=tm, pos0=pos0, alpha=alpha),
        grid=(b, t // tm),
        in_specs=[pl.BlockSpec((None, tm, d), row), pl.BlockSpec((None, POOL_PAD, dp), per_b),
                  pl.BlockSpec((d, dp), const), pl.BlockSpec((d, dp), const),
                  pl.BlockSpec((len(POOL_WINDOWS), grp, grp), lambda bi, ti: (0, 0, 0)),
                  pl.BlockSpec((1, dp), const), pl.BlockSpec((dp, d), const),
                  pl.BlockSpec((1, d), const), pl.BlockSpec((1, d), const)],
        out_specs=[pl.BlockSpec((None, tm, d), row), pl.BlockSpec((None, POOL_PAD, dp), per_b)],
        out_shape=(jax.ShapeDtypeStruct((b, t, d), F32), jax.ShapeDtypeStruct((b, POOL_PAD, dp), F32)),
        scratch_shapes=[pltpu.VMEM((POOL_PAD + tm, dp), F32), pltpu.VMEM((tm, dp), F32)],
        compiler_params=_cparams(("parallel", "arbitrary")),
        name="c_pool",
    )(x, prev, wu, wg, w_grp.astype(BF16), scale.reshape(1, dp), w_out.astype(BF16), lg, lb)
    return xn, st[:, POOL_PAD - POOL_HIST:]


def kernel(x_prompt, x_sample, cache_k, cache_v, cache_kidx, state_conv, state_pool, w_in_a, w_out_a, w_in_b, conv_w_b, conv_bias_b, norm_g_b, norm_b_b, w_out_b, w_in_c, w_grp_c, scale_c, w_out_c, ln_g, ln_b):
    depth = ln_g.shape[0]
    alpha = (2.0 * depth) ** 0.25
    xp, xs = x_prompt, x_sample
    bp, tp, d = xp.shape
    bs, ts, _ = xs.shape
    past = cache_k.shape[2]
    da = N_HEADS * HEAD_DIM

    tm_p = min(256, tp)
    tm_s = min(256, bs * ts)
    rope_p = _rope_tables(jnp.arange(tp))
    rope_s = _rope_tables(past + jnp.arange(bs * ts) % ts)

    kp, vp, kip, cvp, plp = [], [], [], [], []
    ksm, vsm, kism, cvs, pls = [], [], [], [], []
    for i in range(depth):
        m, j = i % 3, i // 3
        lg, lb = ln_g[i].reshape(1, d), ln_b[i].reshape(1, d)
        if m == 0:
            xp, k1, v1, ki1 = _attn_layer(xp, rope_p, None, w_in_a[j], w_out_a[j], lg, lb, alpha,
                                          tm=tm_p, tq=min(256, tp), tk=min(512, tp))
            caches = (cache_k[j].reshape(bs, past, da), cache_v[j].reshape(bs, past, da), cache_kidx[j])
            xs, k2, v2, ki2 = _attn_layer(xs, rope_s, caches, w_in_a[j], w_out_a[j], lg, lb, alpha,
                                          tm=tm_s, tq=ts, tk=None)
            kp.append(k1); vp.append(v1); kip.append(ki1)
            ksm.append(k2); vsm.append(v2); kism.append(ki2)
        elif m == 1:
            args = (w_in_b[j], conv_w_b[j], conv_bias_b[j], norm_g_b[j], norm_b_b[j], w_out_b[j], lg, lb, alpha)
            xp, c1 = _conv_layer(xp, jnp.zeros((bp, CONV_WIDTH - 1, d), F32), *args, tm=min(256, tp))
            xs, c2 = _conv_layer(xs, state_conv[j], *args, tm=min(256, ts))
            cvp.append(c1); cvs.append(c2)
        else:
            args = (w_in_c[j], w_grp_c[j], scale_c[j], w_out_c[j], lg, lb, alpha)
            xp, p1 = _pool_layer(xp, jnp.zeros((bp, POOL_HIST, d), F32), 0, *args, tm=min(256, tp))
            xs, p2 = _pool_layer(xs, state_pool[j], past, *args, tm=min(256, ts))
            plp.append(p1); pls.append(p2)
    return (xp, xs,
            jnp.stack(kp), jnp.stack(vp), jnp.stack(kip), jnp.stack(cvp), jnp.stack(plp),
            jnp.stack(ksm), jnp.stack(vsm), jnp.stack(kism), jnp.stack(cvs), jnp.stack(pls))
```

```python
import functools

import jax
import jax.numpy as jnp
import numpy as np
from jax import lax
from jax.experimental import pallas as pl
from jax.experimental.pallas import tpu as pltpu

F32, BF16, I32 = jnp.float32, jnp.bfloat16, jnp.int32

LANES = 128
CHUNK = 64
N_HEADS, HEAD_DIM = 8, 128
IDX_HEADS, IDX_DIM = 4, 64
TOPK_MAX = 256
ROPE_THETA = 500000.0
CONV_WIDTH = 31
CONV_HIST = 32
POOL_WINDOWS = (2, 4, 8, 16)
POOL_HIST = 15
POOL_PAD = 16
LN_EPS = 1e-5
NEG_BIG = -1e30
LOG2E = float(np.log2(np.e))
INT_MIN = -(2 ** 31)
NINF_KEY = int(np.int32(np.uint32(0xFF800000) ^ np.uint32(0x7FFFFFFF)))
COUNT_ROWS = 128
SOFTMAX_ROWS = 32
QK_AHEAD = 2
VMEM_LIMIT = 56 * 1024 * 1024


def _cparams(sem):
    return pltpu.CompilerParams(dimension_semantics=sem, vmem_limit_bytes=VMEM_LIMIT)


def _dot(a, b):
    return jnp.dot(a, b, preferred_element_type=F32)


def _dot_nt(a, b):
    return lax.dot_general(a, b, (((1,), (1,)), ((), ())), preferred_element_type=F32)


def _layer_norm(z, g, b):
    mu = jnp.mean(z, axis=-1, keepdims=True)
    zc = z - mu
    var = jnp.mean(zc * zc, axis=-1, keepdims=True)
    return zc * lax.rsqrt(var + LN_EPS) * g + b


def _silu(x):
    return x * jax.nn.sigmoid(x)


def _rope(x, c, sa, sb, half):
    return x * c + pltpu.roll(x, LANES - half, 1) * sa + pltpu.roll(x, half, 1) * sb


def _rope_tables(pos):
    def tab(dim):
        r = dim // 4
        half = r // 2
        inv = ROPE_THETA ** (-jnp.arange(half, dtype=F32) * 2.0 / r)
        ang = pos.astype(F32)[:, None] * inv[None, :]
        cos, sin = jnp.cos(ang), jnp.sin(ang)
        n = pos.shape[0]
        c = jnp.concatenate([cos, cos, jnp.ones((n, LANES - r), F32)], axis=1)
        sa = jnp.concatenate([-sin, jnp.zeros((n, LANES - half), F32)], axis=1)
        sb = jnp.concatenate([jnp.zeros((n, half), F32), sin, jnp.zeros((n, LANES - r), F32)], axis=1)
        return [c, sa, sb]
    return jnp.stack(tab(HEAD_DIM) + tab(IDX_DIM), axis=0)


IDX_SLABS = IDX_HEADS + 1


def _aproj_kernel(x_ref, rope_ref, wq_ref, wk_ref, wv_ref, wg_ref, wi_ref,
                  k32_ref, v32_ref, idx32_ref, q_ref, kb_ref, vb_ref, sg_ref, qis_ref, kib_ref):
    xb = x_ref[...].astype(BF16)
    cq, saq, sbq = rope_ref[0], rope_ref[1], rope_ref[2]
    ci, sai, sbi = rope_ref[3], rope_ref[4], rope_ref[5]

    hq = _dot(xb, wq_ref[...])
    for h in range(N_HEADS):
        sl = slice(h * HEAD_DIM, (h + 1) * HEAD_DIM)
        q_ref[:, sl] = (_rope(hq[:, sl], cq, saq, sbq, HEAD_DIM // 8) * (HEAD_DIM ** -0.5 * LOG2E)).astype(BF16)
    hk = _dot(xb, wk_ref[...])
    for h in range(N_HEADS):
        sl = slice(h * HEAD_DIM, (h + 1) * HEAD_DIM)
        kh = _rope(hk[:, sl], cq, saq, sbq, HEAD_DIM // 8)
        k32_ref[:, sl] = kh
        kb_ref[:, sl] = kh.astype(BF16)
    hv = _dot(xb, wv_ref[...])
    v32_ref[...] = hv
    vb_ref[...] = hv.astype(BF16)
    sg_ref[...] = _silu(_dot(xb, wg_ref[...])).astype(BF16)

    hi = _dot(xb, wi_ref[...])
    for h in range(IDX_HEADS):
        sl = slice(h * LANES, (h + 1) * LANES)
        qis_ref[:, sl] = (_rope(hi[:, sl], ci, sai, sbi, IDX_DIM // 8) * (IDX_DIM ** -0.5)).astype(BF16)
    last = _rope(hi[:, IDX_HEADS * LANES:], ci, sai, sbi, IDX_DIM // 8)
    idx32_ref[...] = last
    lane = lax.broadcasted_iota(I32, last.shape, 1)
    kib_ref[...] = jnp.where(lane < IDX_DIM, last, 0.0).astype(BF16)


def _split_a_weights(w_in):
    d = w_in.shape[0]
    da = N_HEADS * HEAD_DIM
    wq, wk, wv, wg = (w_in[:, i * da:(i + 1) * da].astype(BF16) for i in range(4))
    o = 4 * da
    slabs = []
    for h in range(IDX_HEADS):
        slabs += [w_in[:, o + h * IDX_DIM:o + (h + 1) * IDX_DIM], jnp.zeros((d, LANES - IDX_DIM), w_in.dtype)]
    o += IDX_HEADS * IDX_DIM
    slabs += [w_in[:, o:o + IDX_DIM + IDX_HEADS], jnp.zeros((d, LANES - IDX_DIM - IDX_HEADS), w_in.dtype)]
    wi = jnp.concatenate(slabs, axis=1).astype(BF16)
    return wq, wk, wv, wg, wi


def _aproj(x2, rope, weights, tm):
    n, d = x2.shape
    da = N_HEADS * HEAD_DIM
    rope_tiles = rope.shape[1] // tm
    row = lambda i: (i, 0)
    const = lambda i: (0, 0)
    wq, wk, wv, wg, wi = weights
    out_shape = (
        jax.ShapeDtypeStruct((n, da), F32), jax.ShapeDtypeStruct((n, da), F32),
        jax.ShapeDtypeStruct((n, LANES), F32),
        jax.ShapeDtypeStruct((n, da), BF16), jax.ShapeDtypeStruct((n, da), BF16),
        jax.ShapeDtypeStruct((n, da), BF16), jax.ShapeDtypeStruct((n, da), BF16),
        jax.ShapeDtypeStruct((n, IDX_HEADS * LANES), BF16), jax.ShapeDtypeStruct((n, LANES), BF16),
    )
    return pl.pallas_call(
        _aproj_kernel,
        grid=(n // tm,),
        in_specs=[pl.BlockSpec((tm, d), row),
                  pl.BlockSpec((6, tm, LANES), lambda i: (0, i % rope_tiles, 0)),
                  pl.BlockSpec((d, da), const), pl.BlockSpec((d, da), const),
                  pl.BlockSpec((d, da), const), pl.BlockSpec((d, da), const),
                  pl.BlockSpec((d, IDX_SLABS * LANES), const)],
        out_specs=[pl.BlockSpec((tm, da), row), pl.BlockSpec((tm, da), row), pl.BlockSpec((tm, LANES), row),
                   pl.BlockSpec((tm, da), row), pl.BlockSpec((tm, da), row), pl.BlockSpec((tm, da), row),
                   pl.BlockSpec((tm, da), row), pl.BlockSpec((tm, IDX_HEADS * LANES), row),
                   pl.BlockSpec((tm, LANES), row)],
        out_shape=out_shape,
        compiler_params=_cparams(("parallel",)),
        name="a_proj",
    )(x2, rope, wq, wk, wv, wg, wi)


def _index_keys(qis_ref, idx32_ref, kib_ref, q_row0, k_col0, adm_base):
    tq = qis_ref.shape[0]
    tk = kib_ref.shape[0]
    kib = kib_ref[...]
    w = idx32_ref[:, IDX_DIM:IDX_DIM + IDX_HEADS] * (IDX_HEADS ** -0.5)
    s = None
    for h in range(IDX_HEADS):
        d = _dot_nt(qis_ref[:, h * LANES:(h + 1) * LANES], kib)
        t = jnp.maximum(d, 0.0) * w[:, h:h + 1]
        s = t if s is None else s + t
    bits = lax.bitcast_convert_type(s, I32)
    key = bits ^ ((bits >> 31) & 0x7FFFFFFF)
    key = jnp.where(key == -1, 0, key)
    kidx = k_col0 + lax.broadcasted_iota(I32, (tq, tk), 1)
    rows = q_row0 + lax.broadcasted_iota(I32, (tq, 1), 0)
    n_adm = adm_base + (rows // CHUNK + 1) * CHUNK
    return jnp.where(kidx < n_adm, key, NINF_KEY)


def _pair_tables(n_q, tq, tk, adm_base):
    qt, kt = [], []
    for q in range(n_q):
        last = (adm_base + ((q * tq + tq - 1) // CHUNK + 1) * CHUNK - 1) // tk
        for k in range(last + 1):
            qt.append(q)
            kt.append(k)
    return jnp.asarray(qt, I32), jnp.asarray(kt, I32)


def _last_key_tile(qt, tq, tk, adm_base):
    return (adm_base + ((qt * tq + tq - 1) // CHUNK + 1) * CHUNK - 1) // tk


def _select_kernel(qt_ref, kt_ref, qis_ref, idx32_ref, kib_ref, bias_ref, keys_ref, tri_ref,
                   *, tq, tk, n_kt, adm_base, topk):
    p = pl.program_id(1)
    qt = qt_ref[p]
    kt = kt_ref[p]
    cols = tk // LANES
    rb = min(tq, COUNT_ROWS)

    @pl.when(p == 0)
    def _tri():
        r = lax.broadcasted_iota(I32, tri_ref.shape, 0)
        c = lax.broadcasted_iota(I32, tri_ref.shape, 1)
        tri_ref[...] = jnp.where(c >= r, 1.0, 0.0).astype(BF16)

    keys_ref[kt] = _index_keys(qis_ref, idx32_ref, kib_ref, qt * tq, kt * tk, adm_base)

    @pl.when(kt == _last_key_tile(qt, tq, tk, adm_base))
    def _search():
        n_tiles = kt + 1

        def count_ge(cand):
            parts = []
            for r0 in range(0, tq, rb):
                cand_r = cand[r0:r0 + rb]

                def body(t, acc, r0=r0, cand_r=cand_r):
                    for c in range(cols):
                        v = keys_ref[t, r0:r0 + rb, c * LANES:(c + 1) * LANES]
                        acc = acc + jnp.where(v >= cand_r, 1, 0)
                    return acc
                parts.append(lax.fori_loop(0, n_tiles, body, jnp.zeros((rb, LANES), I32)))
            acc = jnp.concatenate(parts, axis=0) if len(parts) > 1 else parts[0]
            total = jnp.sum(acc.astype(F32), axis=1, keepdims=True).astype(I32)
            return jnp.broadcast_to(total, (tq, LANES))

        zero = jnp.zeros((tq, LANES), I32)
        base = jnp.where(count_ge(zero) >= topk, zero, INT_MIN)

        def value_bit(i, base):
            cand = base | jnp.left_shift(jnp.int32(1), 30 - i)
            return jnp.where(count_ge(cand) >= topk, cand, base)
        thr = lax.fori_loop(0, 31, value_bit, base)

        all_adm = thr <= NINF_KEY
        thr = jnp.maximum(thr, NINF_KEY)
        need = jnp.where(all_adm, 0, topk - count_ge(thr + 1)).astype(F32)
        thr_t = jnp.tile(thr, (1, cols))
        need_t = jnp.tile(need, (1, cols))

        def emit(t, run):
            key = keys_ref[t]
            eq = key == thr_t
            pref = _dot(jnp.where(eq, 1.0, 0.0).astype(BF16), tri_ref[...])
            rank = jnp.tile(run, (1, cols)) + pref[:, :tk]
            sel = (key > thr_t) | (eq & (rank <= need_t))
            bias_ref[t] = jnp.where(sel, 0.0, NEG_BIG).astype(BF16)
            return run + pref[:, tk:]
        lax.fori_loop(0, n_tiles, emit, jnp.zeros((tq, LANES), F32))

        def fill(t, carry):
            bias_ref[t] = jnp.full((tq, tk), NEG_BIG, BF16)
            return carry
        lax.fori_loop(n_tiles, n_kt, fill, 0)


def _select(qis, idx32, kib, *, tq, tk, adm_base, topk):
    b, t, _ = qis.shape
    n_kt = kib.shape[1] // tk
    n_q = t // tq
    qt_tab, kt_tab = _pair_tables(n_q, tq, tk, adm_base)
    qmap = lambda bi, p, qt, kt: (bi, qt[p], 0)
    kmap = lambda bi, p, qt, kt: (bi, kt[p], 0)
    kern = functools.partial(_select_kernel, tq=tq, tk=tk, n_kt=n_kt, adm_base=adm_base, topk=topk)
    return pl.pallas_call(
        kern,
        grid_spec=pltpu.PrefetchScalarGridSpec(
            num_scalar_prefetch=2,
            grid=(b, int(qt_tab.shape[0])),
            in_specs=[pl.BlockSpec((None, tq, IDX_HEADS * LANES), qmap),
                      pl.BlockSpec((None, tq, LANES), qmap),
                      pl.BlockSpec((None, tk, LANES), kmap)],
            out_specs=pl.BlockSpec((None, None, n_kt, tq, tk), lambda bi, p, qt, kt: (bi, qt[p], 0, 0, 0)),
            scratch_shapes=[pltpu.VMEM((n_kt, tq, tk), I32), pltpu.VMEM((tk, tk + LANES), BF16)]),
        out_shape=jax.ShapeDtypeStruct((b, n_q, n_kt, tq, tk), BF16),
        compiler_params=_cparams(("parallel", "arbitrary")),
        name="a_select",
    )(qt_tab, kt_tab, qis, idx32, kib)


def _attn_kernel(qt_ref, kt_ref, q_ref, sg_ref, x_ref, k_ref, v_ref, bias_ref, wo_ref, lg_ref, lb_ref,
                 xn_ref, m_ref, l_ref, acc_ref, s_ref, p_ref, b32_ref, a_ref, *, tq, tk, adm_base, alpha):
    p = pl.program_id(1)
    qt = qt_ref[p]
    kt = kt_ref[p]
    cols = tk // LANES
    rb = min(tq, SOFTMAX_ROWS)

    @pl.when(kt == 0)
    def _init():
        m_ref[...] = jnp.full(m_ref.shape, NEG_BIG, F32)
        l_ref[...] = jnp.zeros(l_ref.shape, F32)
        acc_ref[...] = jnp.zeros(acc_ref.shape, F32)

    b32_ref[...] = bias_ref[...].astype(F32)

    def lane_blocks(a):
        return [a[:, c * LANES:(c + 1) * LANES] for c in range(cols)]

    def head(h):
        return slice(h * HEAD_DIM, (h + 1) * HEAD_DIM)

    for h in range(min(QK_AHEAD, N_HEADS)):
        s_ref[h] = _dot_nt(q_ref[:, head(h)], k_ref[:, head(h)])
    for h in range(N_HEADS):
        sl = head(h)
        if h + QK_AHEAD < N_HEADS:
            s_ref[h + QK_AHEAD] = _dot_nt(q_ref[:, head(h + QK_AHEAD)], k_ref[:, head(h + QK_AHEAD)])
        for r0 in range(0, tq, rb):
            rows = slice(r0, r0 + rb)
            blocks = lane_blocks(s_ref[h, rows, :] + b32_ref[rows, :])
            m_prev = m_ref[h, rows]
            m_new = jnp.maximum(m_prev, jnp.max(functools.reduce(jnp.maximum, blocks), axis=1, keepdims=True))
            a_ref[h, rows] = jnp.exp2(m_prev - m_new)
            m_ref[h, rows] = m_new
        for r0 in range(0, tq, rb):
            rows = slice(r0, r0 + rb)
            m_r = m_ref[h, rows]
            blocks = [jnp.exp2(blk - m_r) for blk in lane_blocks(s_ref[h, rows, :] + b32_ref[rows, :])]
            p_ref[h, rows, :] = jnp.concatenate(blocks, axis=1).astype(BF16)
            l_ref[h, rows] = (a_ref[h, rows] * l_ref[h, rows]
                              + jnp.sum(functools.reduce(jnp.add, blocks), axis=1, keepdims=True))
        acc_ref[:, sl] = a_ref[h] * acc_ref[:, sl] + _dot(p_ref[h], v_ref[:, sl])

    @pl.when(kt == _last_key_tile(qt, tq, tk, adm_base))
    def _finish():
        for h in range(N_HEADS):
            sl = slice(h * HEAD_DIM, (h + 1) * HEAD_DIM)
            acc_ref[:, sl] = acc_ref[:, sl] / l_ref[h]
        y = _dot((acc_ref[...] * sg_ref[...].astype(F32)).astype(BF16), wo_ref[...])
        xn_ref[...] = _layer_norm(alpha * x_ref[...] + y, lg_ref[...], lb_ref[...])


def _attend(q, sg, x, kb, vb, bias, wo, lg, lb, *, tq, tk, adm_base, alpha):
    b, t, da = q.shape
    d = x.shape[-1]
    qt_tab, kt_tab = _pair_tables(t // tq, tq, tk, adm_base)
    qmap = lambda bi, p, qt, kt: (bi, qt[p], 0)
    kmap = lambda bi, p, qt, kt: (bi, kt[p], 0)
    const = lambda bi, p, qt, kt: (0, 0)
    kern = functools.partial(_attn_kernel, tq=tq, tk=tk, adm_base=adm_base, alpha=alpha)
    return pl.pallas_call(
        kern,
        grid_spec=pltpu.PrefetchScalarGridSpec(
            num_scalar_prefetch=2,
            grid=(b, int(qt_tab.shape[0])),
            in_specs=[pl.BlockSpec((None, tq, da), qmap), pl.BlockSpec((None, tq, da), qmap),
                      pl.BlockSpec((None, tq, d), qmap),
                      pl.BlockSpec((None, tk, da), kmap), pl.BlockSpec((None, tk, da), kmap),
                      pl.BlockSpec((None, None, None, tq, tk), lambda bi, p, qt, kt: (bi, qt[p], kt[p], 0, 0)),
                      pl.BlockSpec((da, d), const), pl.BlockSpec((1, d), const), pl.BlockSpec((1, d), const)],
            out_specs=pl.BlockSpec((None, tq, d), qmap),
            scratch_shapes=[pltpu.VMEM((N_HEADS, tq, LANES), F32), pltpu.VMEM((N_HEADS, tq, LANES), F32),
                            pltpu.VMEM((tq, da), F32), pltpu.VMEM((N_HEADS, tq, tk), F32),
                            pltpu.VMEM((N_HEADS, tq, tk), BF16), pltpu.VMEM((tq, tk), F32),
                            pltpu.VMEM((N_HEADS, tq, LANES), F32)]),
        out_shape=jax.ShapeDtypeStruct((b, t, d), F32),
        compiler_params=_cparams(("parallel", "arbitrary")),
        name="a_attend",
    )(qt_tab, kt_tab, q, sg, x, kb, vb, bias, wo, lg, lb)


def _attn_layer(x, rope, past, w_in, w_out, lg, lb, alpha, *, tm, tq, tk):
    b, t, d = x.shape
    da = N_HEADS * HEAD_DIM
    k32, v32, idx32, q, kb, vb, sg, qis, kib = _aproj(x.reshape(b * t, d), rope, _split_a_weights(w_in), tm)
    r3 = lambda a: a.reshape(b, t, a.shape[-1])
    k32, v32, idx32, q, kb, vb, sg, qis, kib = map(r3, (k32, v32, idx32, q, kb, vb, sg, qis, kib))
    if past is None:
        adm_base, kb_all, vb_all, kib_all = 0, kb, vb, kib
        total = t
    else:
        ck, cv, cki = past
        adm_base = ck.shape[1]
        total = adm_base + t
        pad = (-total) % LANES
        zpad = lambda w: jnp.zeros((b, pad, w), BF16)
        kb_all = jnp.concatenate([ck.astype(BF16), kb, zpad(da)], axis=1)
        vb_all = jnp.concatenate([cv.astype(BF16), vb, zpad(da)], axis=1)
        cki_pad = jnp.concatenate([cki.astype(BF16), jnp.zeros((b, adm_base, LANES - IDX_DIM), BF16)], axis=2)
        kib_all = jnp.concatenate([cki_pad, kib, zpad(LANES)], axis=1)
        tk = kb_all.shape[1]
    topk = min(TOPK_MAX, total // 4)
    bias = _select(qis, idx32, kib_all, tq=tq, tk=tk, adm_base=adm_base, topk=topk)
    xn = _attend(q, sg, x, kb_all, vb_all, bias, w_out.astype(BF16), lg, lb,
                 tq=tq, tk=tk, adm_base=adm_base, alpha=alpha)
    return xn, k32.reshape(b, t, N_HEADS, HEAD_DIM), v32.reshape(b, t, N_HEADS, HEAD_DIM), idx32[:, :, :IDX_DIM]


def _conv_kernel(x_ref, prev_ref, wa_ref, wb_ref, wg_ref, cw_ref, cb_ref, ng_ref, nb_ref, wo_ref,
                 lg_ref, lb_ref, xn_ref, st_ref, ext_ref, c_ref, *, tm, alpha):
    t = pl.program_id(1)

    @pl.when(t == 0)
    def _first():
        ext_ref[0:CONV_HIST] = prev_ref[...]

    @pl.when(t > 0)
    def _carry():
        ext_ref[0:CONV_HIST] = ext_ref[tm:tm + CONV_HIST]

    x = x_ref[...]
    xb = x.astype(BF16)
    ext_ref[CONV_HIST:CONV_HIST + tm] = _dot(xb, wa_ref[...]) * jax.nn.sigmoid(_dot(xb, wb_ref[...]))
    st_ref[...] = ext_ref[tm:tm + CONV_HIST]

    off = CONV_HIST - (CONV_WIDTH - 1)
    for cs in range(x.shape[1] // LANES):
        sl = slice(cs * LANES, (cs + 1) * LANES)
        acc = jnp.broadcast_to(cb_ref[:, sl], (tm, LANES))
        for w in range(CONV_WIDTH):
            acc = acc + ext_ref[off + w:off + w + tm, sl] * cw_ref[w:w + 1, sl]
        c_ref[:, sl] = acc

    cn = _silu(_layer_norm(c_ref[...], ng_ref[...], nb_ref[...]))
    y = _dot((cn * _silu(_dot(xb, wg_ref[...]))).astype(BF16), wo_ref[...])
    xn_ref[...] = _layer_norm(alpha * x + y, lg_ref[...], lb_ref[...])


def _conv_layer(x, prev, w_in, conv_w, conv_b, n_g, n_b, w_out, lg, lb, alpha, *, tm):
    b, t, d = x.shape
    dc = conv_w.shape[1]
    assert t >= CONV_HIST and t % tm == 0
    wa, wb, wg = (w_in[:, i * dc:(i + 1) * dc].astype(BF16) for i in range(3))
    prev = jnp.concatenate([jnp.zeros((b, CONV_HIST - (CONV_WIDTH - 1), dc), F32), prev], axis=1)
    cw = jnp.concatenate([conv_w, jnp.zeros((CONV_HIST - CONV_WIDTH, dc), F32)], axis=0)
    row = lambda bi, ti: (bi, ti, 0)
    per_b = lambda bi, ti: (bi, 0, 0)
    const = lambda bi, ti: (0, 0)
    vec = pl.BlockSpec((1, dc), const)
    xn, st = pl.pallas_call(
        functools.partial(_conv_kernel, tm=tm, alpha=alpha),
        grid=(b, t // tm),
        in_specs=[pl.BlockSpec((None, tm, d), row), pl.BlockSpec((None, CONV_HIST, dc), per_b),
                  pl.BlockSpec((d, dc), const), pl.BlockSpec((d, dc), const), pl.BlockSpec((d, dc), const),
                  pl.BlockSpec((CONV_HIST, dc), const), vec, vec, vec,
                  pl.BlockSpec((dc, d), const), pl.BlockSpec((1, d), const), pl.BlockSpec((1, d), const)],
        out_specs=[pl.BlockSpec((None, tm, d), row), pl.BlockSpec((None, CONV_HIST, dc), per_b)],
        out_shape=(jax.ShapeDtypeStruct((b, t, d), F32), jax.ShapeDtypeStruct((b, CONV_HIST, dc), F32)),
        scratch_shapes=[pltpu.VMEM((CONV_HIST + tm, dc), F32), pltpu.VMEM((tm, dc), F32)],
        compiler_params=_cparams(("parallel", "arbitrary")),
        name="b_conv",
    )(x, prev, wa, wb, wg, cw, conv_b.reshape(1, dc), n_g.reshape(1, dc), n_b.reshape(1, dc),
      w_out.astype(BF16), lg, lb)
    return xn, st[:, CONV_HIST - (CONV_WIDTH - 1):]


def _pool_kernel(x_ref, prev_ref, wu_ref, wg_ref, wgrp_ref, sc_ref, wo_ref, lg_ref, lb_ref,
                 xn_ref, st_ref, ext_ref, mix_ref, *, tm, pos0, alpha):
    t = pl.program_id(1)

    @pl.when(t == 0)
    def _first():
        ext_ref[0:POOL_PAD] = prev_ref[...]

    @pl.when(t > 0)
    def _carry():
        ext_ref[0:POOL_PAD] = ext_ref[tm:tm + POOL_PAD]

    x = x_ref[...]
    xb = x.astype(BF16)
    ext_ref[POOL_PAD:POOL_PAD + tm] = _dot(xb, wu_ref[...])
    st_ref[...] = ext_ref[tm:tm + POOL_PAD]

    pos = pos0 + t * tm + lax.broadcasted_iota(I32, (tm, 1), 0)
    grp = x.shape[1] // len(POOL_WINDOWS)
    for gi, w in enumerate(POOL_WINDOWS):
        sl = slice(gi * grp, (gi + 1) * grp)
        u = ext_ref[POOL_PAD:POOL_PAD + tm, sl]
        ws = u
        for i in range(1, w):
            ws = ws + ext_ref[POOL_PAD - i:POOL_PAD - i + tm, sl]
        cnt = jnp.minimum(pos + 1, w).astype(F32)
        dlt = ws / cnt - u
        mix_ref[:, sl] = _dot(dlt.astype(BF16), wgrp_ref[gi]) * sc_ref[:, sl]

    y = _dot((mix_ref[...] * _silu(_dot(xb, wg_ref[...]))).astype(BF16), wo_ref[...])
    xn_ref[...] = _layer_norm(alpha * x + y, lg_ref[...], lb_ref[...])


def _pool_layer(x, prev, pos0, w_in, w_grp, scale, w_out, lg, lb, alpha, *, tm):
    b, t, d = x.shape
    dp = scale.shape[0]
    grp = dp // len(POOL_WINDOWS)
    assert t >= POOL_PAD and t % tm == 0
    wu, wg = (w_in[:, i * dp:(i + 1) * dp].astype(BF16) for i in range(2))
    prev = jnp.concatenate([jnp.zeros((b, POOL_PAD - POOL_HIST, dp), F32), prev], axis=1)
    row = lambda bi, ti: (bi, ti, 0)
    per_b = lambda bi, ti: (bi, 0, 0)
    const = lambda bi, ti: (0, 0)
    xn, st = pl.pallas_call(
        functools.partial(_pool_kernel, tm=tm, pos0=pos0, alpha=alpha),
        grid=(b, t // tm),
        in_specs=[pl.BlockSpec((None, tm, d), row), pl.BlockSpec((None, POOL_PAD, dp), per_b),
                  pl.BlockSpec((d, dp), const), pl.BlockSpec((d, dp), const),
                  pl.BlockSpec((len(POOL_WINDOWS), grp, grp), lambda bi, ti: (0, 0, 0)),
                  pl.BlockSpec((1, dp), const), pl.BlockSpec((dp, d), const),
                  pl.BlockSpec((1, d), const), pl.BlockSpec((1, d), const)],
        out_specs=[pl.BlockSpec((None, tm, d), row), pl.BlockSpec((None, POOL_PAD, dp), per_b)],
        out_shape=(jax.ShapeDtypeStruct((b, t, d), F32), jax.ShapeDtypeStruct((b, POOL_PAD, dp), F32)),
        scratch_shapes=[pltpu.VMEM((POOL_PAD + tm, dp), F32), pltpu.VMEM((tm, dp), F32)],
        compiler_params=_cparams(("parallel", "arbitrary")),
        name="c_pool",
    )(x, prev, wu, wg, w_grp.astype(BF16), scale.reshape(1, dp), w_out.astype(BF16), lg, lb)
    return xn, st[:, POOL_PAD - POOL_HIST:]


def kernel(x_prompt, x_sample, cache_k, cache_v, cache_kidx, state_conv, state_pool, w_in_a, w_out_a, w_in_b, conv_w_b, conv_bias_b, norm_g_b, norm_b_b, w_out_b, w_in_c, w_grp_c, scale_c, w_out_c, ln_g, ln_b):
    depth = ln_g.shape[0]
    alpha = (2.0 * depth) ** 0.25
    xp, xs = x_prompt, x_sample
    bp, tp, d = xp.shape
    bs, ts, _ = xs.shape
    past = cache_k.shape[2]
    da = N_HEADS * HEAD_DIM

    tm_p = min(256, tp)
    tm_s = min(256, bs * ts)
    rope_p = _rope_tables(jnp.arange(tp))
    rope_s = _rope_tables(past + jnp.arange(bs * ts) % ts)

    kp, vp, kip, cvp, plp = [], [], [], [], []
    ksm, vsm, kism, cvs, pls = [], [], [], [], []
    for i in range(depth):
        m, j = i % 3, i // 3
        lg, lb = ln_g[i].reshape(1, d), ln_b[i].reshape(1, d)
        if m == 0:
            xp, k1, v1, ki1 = _attn_layer(xp, rope_p, None, w_in_a[j], w_out_a[j], lg, lb, alpha,
                                          tm=tm_p, tq=min(256, tp), tk=min(512, tp))
            caches = (cache_k[j].reshape(bs, past, da), cache_v[j].reshape(bs, past, da), cache_kidx[j])
            xs, k2, v2, ki2 = _attn_layer(xs, rope_s, caches, w_in_a[j], w_out_a[j], lg, lb, alpha,
                                          tm=tm_s, tq=ts, tk=None)
            kp.append(k1); vp.append(v1); kip.append(ki1)
            ksm.append(k2); vsm.append(v2); kism.append(ki2)
        elif m == 1:
            args = (w_in_b[j], conv_w_b[j], conv_bias_b[j], norm_g_b[j], norm_b_b[j], w_out_b[j], lg, lb, alpha)
            xp, c1 = _conv_layer(xp, jnp.zeros((bp, CONV_WIDTH - 1, d), F32), *args, tm=min(256, tp))
            xs, c2 = _conv_layer(xs, state_conv[j], *args, tm=min(256, ts))
            cvp.append(c1); cvs.append(c2)
        else:
            args = (w_in_c[j], w_grp_c[j], scale_c[j], w_out_c[j], lg, lb, alpha)
            xp, p1 = _pool_layer(xp, jnp.zeros((bp, POOL_HIST, d), F32), 0, *args, tm=min(256, tp))
            xs, p2 = _pool_layer(xs, state_pool[j], past, *args, tm=min(256, ts))
            plp.append(p1); pls.append(p2)
    return (xp, xs,
            jnp.stack(kp), jnp.stack(vp), jnp.stack(kip), jnp.stack(cvp), jnp.stack(plp),
            jnp.stack(ksm), jnp.stack(vsm), jnp.stack(kism), jnp.stack(cvs), jnp.stack(pls))
```

```python
import functools

import jax
import jax.numpy as jnp
import numpy as np
from jax import lax
from jax.experimental import pallas as pl
from jax.experimental.pallas import tpu as pltpu

F32, BF16, I32 = jnp.float32, jnp.bfloat16, jnp.int32

LANES = 128
SUBLANES = 8
CHUNK = 64
N_HEADS, HEAD_DIM = 8, 128
IDX_HEADS, IDX_DIM = 4, 64
TOPK_MAX = 256
ROPE_THETA = 500000.0
CONV_WIDTH = 31
CONV_HIST = 32
POOL_WINDOWS = (2, 4, 8, 16)
POOL_HIST = 15
POOL_PAD = 16
LN_EPS = 1e-5
NEG_BIG = -1e30
LOG2E = float(np.log2(np.e))
INT_MIN = -(2 ** 31)
NINF_KEY = int(np.int32(np.uint32(0xFF800000) ^ np.uint32(0x7FFFFFFF)))
SOFTMAX_ROWS = 32
CONV_ROWS = 128
QK_AHEAD = 2
VMEM_LIMIT = 56 * 1024 * 1024


def _cparams(sem):
    return pltpu.CompilerParams(dimension_semantics=sem, vmem_limit_bytes=VMEM_LIMIT)


def _dot(a, b):
    return jnp.dot(a, b, preferred_element_type=F32)


def _dot_nt(a, b):
    return lax.dot_general(a, b, (((1,), (1,)), ((), ())), preferred_element_type=F32)


def _layer_norm(z, g, b):
    mu = jnp.mean(z, axis=-1, keepdims=True)
    zc = z - mu
    var = jnp.mean(zc * zc, axis=-1, keepdims=True)
    return zc * lax.rsqrt(var + LN_EPS) * g + b


def _silu(x):
    return x * jax.nn.sigmoid(x)


def _rope(x, c, sa, sb, half):
    return x * c + pltpu.roll(x, LANES - half, 1) * sa + pltpu.roll(x, half, 1) * sb


def _rope_tables(pos):
    def tab(dim):
        r = dim // 4
        half = r // 2
        inv = ROPE_THETA ** (-jnp.arange(half, dtype=F32) * 2.0 / r)
        ang = pos.astype(F32)[:, None] * inv[None, :]
        cos, sin = jnp.cos(ang), jnp.sin(ang)
        n = pos.shape[0]
        c = jnp.concatenate([cos, cos, jnp.ones((n, LANES - r), F32)], axis=1)
        sa = jnp.concatenate([-sin, jnp.zeros((n, LANES - half), F32)], axis=1)
        sb = jnp.concatenate([jnp.zeros((n, half), F32), sin, jnp.zeros((n, LANES - r), F32)], axis=1)
        return [c, sa, sb]
    return jnp.stack(tab(HEAD_DIM) + tab(IDX_DIM), axis=0)


IDX_SLABS = IDX_HEADS + 1


def _aproj_kernel(x_ref, rope_ref, wq_ref, wk_ref, wv_ref, wg_ref, wi_ref, wwt_ref,
                  k32_ref, v32_ref, idx32_ref, q_ref, kb_ref, vb_ref, sg_ref, qis_ref, kib_ref, wt_ref):
    xb = x_ref[...].astype(BF16)
    wt_ref[...] = _dot_nt(wwt_ref[...], xb) * (IDX_HEADS ** -0.5)
    cq, saq, sbq = rope_ref[0], rope_ref[1], rope_ref[2]
    ci, sai, sbi = rope_ref[3], rope_ref[4], rope_ref[5]

    hq = _dot(xb, wq_ref[...])
    for h in range(N_HEADS):
        sl = slice(h * HEAD_DIM, (h + 1) * HEAD_DIM)
        q_ref[:, sl] = (_rope(hq[:, sl], cq, saq, sbq, HEAD_DIM // 8) * (HEAD_DIM ** -0.5 * LOG2E)).astype(BF16)
    hk = _dot(xb, wk_ref[...])
    for h in range(N_HEADS):
        sl = slice(h * HEAD_DIM, (h + 1) * HEAD_DIM)
        kh = _rope(hk[:, sl], cq, saq, sbq, HEAD_DIM // 8)
        k32_ref[:, sl] = kh
        kb_ref[:, sl] = kh.astype(BF16)
    hv = _dot(xb, wv_ref[...])
    v32_ref[...] = hv
    vb_ref[...] = hv.astype(BF16)
    sg_ref[...] = _silu(_dot(xb, wg_ref[...])).astype(BF16)

    hi = _dot(xb, wi_ref[...])
    for h in range(IDX_HEADS):
        sl = slice(h * LANES, (h + 1) * LANES)
        qis_ref[:, sl] = (_rope(hi[:, sl], ci, sai, sbi, IDX_DIM // 8) * (IDX_DIM ** -0.5)).astype(BF16)
    last = _rope(hi[:, IDX_HEADS * LANES:], ci, sai, sbi, IDX_DIM // 8)
    idx32_ref[...] = last
    lane = lax.broadcasted_iota(I32, last.shape, 1)
    kib_ref[...] = jnp.where(lane < IDX_DIM, last, 0.0).astype(BF16)


def _split_a_weights(w_in):
    d = w_in.shape[0]
    da = N_HEADS * HEAD_DIM
    wq, wk, wv, wg = (w_in[:, i * da:(i + 1) * da].astype(BF16) for i in range(4))
    o = 4 * da
    slabs = []
    for h in range(IDX_HEADS):
        slabs += [w_in[:, o + h * IDX_DIM:o + (h + 1) * IDX_DIM], jnp.zeros((d, LANES - IDX_DIM), w_in.dtype)]
    o += IDX_HEADS * IDX_DIM
    slabs += [w_in[:, o:o + IDX_DIM + IDX_HEADS], jnp.zeros((d, LANES - IDX_DIM - IDX_HEADS), w_in.dtype)]
    wi = jnp.concatenate(slabs, axis=1).astype(BF16)
    wwt = jnp.concatenate([w_in[:, o + IDX_DIM:o + IDX_DIM + IDX_HEADS].T,
                           jnp.zeros((SUBLANES - IDX_HEADS, d), w_in.dtype)], axis=0).astype(BF16)
    return wq, wk, wv, wg, wi, wwt


def _aproj(x2, rope, weights, tm):
    n, d = x2.shape
    da = N_HEADS * HEAD_DIM
    rope_tiles = rope.shape[1] // tm
    row = lambda i: (i, 0)
    const = lambda i: (0, 0)
    wq, wk, wv, wg, wi, wwt = weights
    out_shape = (
        jax.ShapeDtypeStruct((n, da), F32), jax.ShapeDtypeStruct((n, da), F32),
        jax.ShapeDtypeStruct((n, LANES), F32),
        jax.ShapeDtypeStruct((n, da), BF16), jax.ShapeDtypeStruct((n, da), BF16),
        jax.ShapeDtypeStruct((n, da), BF16), jax.ShapeDtypeStruct((n, da), BF16),
        jax.ShapeDtypeStruct((n, IDX_HEADS * LANES), BF16), jax.ShapeDtypeStruct((n, LANES), BF16),
        jax.ShapeDtypeStruct((SUBLANES, n), F32),
    )
    return pl.pallas_call(
        _aproj_kernel,
        grid=(n // tm,),
        in_specs=[pl.BlockSpec((tm, d), row),
                  pl.BlockSpec((6, tm, LANES), lambda i: (0, i % rope_tiles, 0)),
                  pl.BlockSpec((d, da), const), pl.BlockSpec((d, da), const),
                  pl.BlockSpec((d, da), const), pl.BlockSpec((d, da), const),
                  pl.BlockSpec((d, IDX_SLABS * LANES), const), pl.BlockSpec((SUBLANES, d), const)],
        out_specs=[pl.BlockSpec((tm, da), row), pl.BlockSpec((tm, da), row), pl.BlockSpec((tm, LANES), row),
                   pl.BlockSpec((tm, da), row), pl.BlockSpec((tm, da), row), pl.BlockSpec((tm, da), row),
                   pl.BlockSpec((tm, da), row), pl.BlockSpec((tm, IDX_HEADS * LANES), row),
                   pl.BlockSpec((tm, LANES), row), pl.BlockSpec((SUBLANES, tm), lambda i: (0, i))],
        out_shape=out_shape,
        compiler_params=_cparams(("parallel",)),
        name="a_proj",
    )(x2, rope, wq, wk, wv, wg, wi, wwt)


def _index_keys(qis_ref, wt_ref, kib_ref, q_row0, k_row0, adm_base):
    tq = qis_ref.shape[0]
    tk = kib_ref.shape[0]
    kib = kib_ref[...]
    s = None
    for h in range(IDX_HEADS):
        d = _dot_nt(kib, qis_ref[:, h * LANES:(h + 1) * LANES])
        t = jnp.maximum(d, 0.0) * wt_ref[h:h + 1, :]
        s = t if s is None else s + t
    bits = lax.bitcast_convert_type(s, I32)
    key = bits ^ ((bits >> 31) & 0x7FFFFFFF)
    key = jnp.where(key == -1, 0, key)
    kidx = k_row0 + lax.broadcasted_iota(I32, (tk, 1), 0)
    qrow = q_row0 + lax.broadcasted_iota(I32, (1, tq), 1)
    n_adm = adm_base + (qrow // CHUNK + 1) * CHUNK
    return jnp.where(kidx < n_adm, key, NINF_KEY)


def _pair_tables(n_q, tq, tk, adm_base):
    qt, kt = [], []
    for q in range(n_q):
        last = (adm_base + ((q * tq + tq - 1) // CHUNK + 1) * CHUNK - 1) // tk
        for k in range(last + 1):
            qt.append(q)
            kt.append(k)
    return jnp.asarray(qt, I32), jnp.asarray(kt, I32)


def _last_key_tile(qt, tq, tk, adm_base):
    return (adm_base + ((qt * tq + tq - 1) // CHUNK + 1) * CHUNK - 1) // tk


def _select_kernel(qt_ref, kt_ref, qis_ref, wt_ref, kib_ref, bias_ref, keys_ref, tri_ref,
                   *, tq, tk, n_kt, adm_base, topk):
    p = pl.program_id(1)
    qt = qt_ref[p]
    kt = kt_ref[p]
    groups = tq // LANES
    slabs = tk // SUBLANES

    @pl.when(p == 0)
    def _tri():
        r = lax.broadcasted_iota(I32, tri_ref.shape, 0)
        c = lax.broadcasted_iota(I32, tri_ref.shape, 1)
        tri_ref[...] = jnp.where(c <= r, 1.0, 0.0).astype(BF16)

    keys_ref[kt] = _index_keys(qis_ref, wt_ref, kib_ref, qt * tq, kt * tk, adm_base)

    @pl.when(kt == _last_key_tile(qt, tq, tk, adm_base))
    def _search():
        n_tiles = kt + 1

        def count_ge(cand):
            cb = [jnp.broadcast_to(cand[:, g * LANES:(g + 1) * LANES], (SUBLANES, LANES)) for g in range(groups)]

            def body(t, accs):
                accs = list(accs)
                for r in range(slabs):
                    for g in range(groups):
                        v = keys_ref[t, r * SUBLANES:(r + 1) * SUBLANES, g * LANES:(g + 1) * LANES]
                        j = 2 * g + r % 2
                        accs[j] = accs[j] + jnp.where(v >= cb[g], 1, 0)
                return tuple(accs)
            zero8 = jnp.zeros((SUBLANES, LANES), I32)
            accs = lax.fori_loop(0, n_tiles, body, (zero8,) * (2 * groups))
            tot = [jnp.sum((accs[2 * g] + accs[2 * g + 1]).astype(F32), axis=0, keepdims=True)
                   for g in range(groups)]
            return jnp.concatenate(tot, axis=1).astype(I32)

        zero = jnp.zeros((1, tq), I32)
        base = jnp.where(count_ge(zero) >= topk, zero, INT_MIN)

        def value_bit(i, base):
            cand = base | jnp.left_shift(jnp.int32(1), 30 - i)
            return jnp.where(count_ge(cand) >= topk, cand, base)
        thr = lax.fori_loop(0, 31, value_bit, base)

        all_adm = thr <= NINF_KEY
        thr = jnp.maximum(thr, NINF_KEY)
        need = jnp.where(all_adm, 0, topk - count_ge(thr + 1)).astype(F32)

        def emit(t, run):
            key = keys_ref[t]
            eq = key == thr
            pref = _dot(tri_ref[...], jnp.where(eq, 1.0, 0.0).astype(BF16))
            sel = (key > thr) | (eq & (run + pref <= need))
            bias_ref[t] = jnp.where(sel, 0.0, NEG_BIG).astype(BF16)
            return run + pref[tk - 1:tk, :]
        lax.fori_loop(0, n_tiles, emit, jnp.zeros((1, tq), F32))

        def fill(t, carry):
            bias_ref[t] = jnp.full((tk, tq), NEG_BIG, BF16)
            return carry
        lax.fori_loop(n_tiles, n_kt, fill, 0)


def _select(qis, wt, kib, *, tq, tk, adm_base, topk):
    b, t, _ = qis.shape
    n_kt = kib.shape[1] // tk
    n_q = t // tq
    qt_tab, kt_tab = _pair_tables(n_q, tq, tk, adm_base)
    kern = functools.partial(_select_kernel, tq=tq, tk=tk, n_kt=n_kt, adm_base=adm_base, topk=topk)
    return pl.pallas_call(
        kern,
        grid_spec=pltpu.PrefetchScalarGridSpec(
            num_scalar_prefetch=2,
            grid=(b, int(qt_tab.shape[0])),
            in_specs=[pl.BlockSpec((None, tq, IDX_HEADS * LANES), lambda bi, p, qt, kt: (bi, qt[p], 0)),
                      pl.BlockSpec((SUBLANES, tq), lambda bi, p, qt, kt: (0, bi * n_q + qt[p])),
                      pl.BlockSpec((None, tk, LANES), lambda bi, p, qt, kt: (bi, kt[p], 0))],
            out_specs=pl.BlockSpec((None, None, n_kt, tk, tq), lambda bi, p, qt, kt: (bi, qt[p], 0, 0, 0)),
            scratch_shapes=[pltpu.VMEM((n_kt, tk, tq), I32), pltpu.VMEM((tk, tk), BF16)]),
        out_shape=jax.ShapeDtypeStruct((b, n_q, n_kt, tk, tq), BF16),
        compiler_params=_cparams(("parallel", "arbitrary")),
        name="a_select",
    )(qt_tab, kt_tab, qis, wt, kib)


def _attn_kernel(qt_ref, kt_ref, q_ref, sg_ref, x_ref, k_ref, v_ref, *rest, cached, tq, tk, adm_base, alpha):
    kc_ref, vc_ref = rest[:2] if cached else (None, None)
    (bias_ref, wo_ref, lg_ref, lb_ref, xn_ref,
     m_ref, l_ref, acc_ref, s_ref, p_ref, b32_ref, a_ref) = rest[2:] if cached else rest
    p = pl.program_id(1)
    qt = qt_ref[p]
    kt = kt_ref[p]
    cols = tk // LANES
    rb = min(tq, SOFTMAX_ROWS)

    def keys_of(h, new_ref, cache_ref):
        sl = slice(h * HEAD_DIM, (h + 1) * HEAD_DIM)
        if cache_ref is None:
            return new_ref[:, sl]
        return jnp.concatenate([cache_ref[:, sl].astype(BF16), new_ref[:, sl]], axis=0)

    @pl.when(kt == 0)
    def _init():
        m_ref[...] = jnp.full(m_ref.shape, NEG_BIG, F32)
        l_ref[...] = jnp.zeros(l_ref.shape, F32)
        acc_ref[...] = jnp.zeros(acc_ref.shape, F32)

    b32_ref[...] = bias_ref[...].astype(F32).T[:tq]

    def lane_blocks(a):
        return [a[:, c * LANES:(c + 1) * LANES] for c in range(cols)]

    def head(h):
        return slice(h * HEAD_DIM, (h + 1) * HEAD_DIM)

    for h in range(min(QK_AHEAD, N_HEADS)):
        s_ref[h] = _dot_nt(q_ref[:, head(h)], keys_of(h, k_ref, kc_ref))
    for h in range(N_HEADS):
        sl = head(h)
        if h + QK_AHEAD < N_HEADS:
            s_ref[h + QK_AHEAD] = _dot_nt(q_ref[:, head(h + QK_AHEAD)], keys_of(h + QK_AHEAD, k_ref, kc_ref))
        for r0 in range(0, tq, rb):
            rows = slice(r0, r0 + rb)
            blocks = lane_blocks(s_ref[h, rows, :] + b32_ref[rows, :])
            m_prev = m_ref[h, rows]
            m_new = jnp.maximum(m_prev, jnp.max(functools.reduce(jnp.maximum, blocks), axis=1, keepdims=True))
            a_ref[h, rows] = jnp.exp2(m_prev - m_new)
            m_ref[h, rows] = m_new
        for r0 in range(0, tq, rb):
            rows = slice(r0, r0 + rb)
            m_r = m_ref[h, rows]
            blocks = [jnp.exp2(blk - m_r) for blk in lane_blocks(s_ref[h, rows, :] + b32_ref[rows, :])]
            p_ref[h, rows, :] = jnp.concatenate(blocks, axis=1).astype(BF16)
            l_ref[h, rows] = (a_ref[h, rows] * l_ref[h, rows]
                              + jnp.sum(functools.reduce(jnp.add, blocks), axis=1, keepdims=True))
        acc_ref[:, sl] = a_ref[h] * acc_ref[:, sl] + _dot(p_ref[h], keys_of(h, v_ref, vc_ref))

    @pl.when(kt == _last_key_tile(qt, tq, tk, adm_base))
    def _finish():
        for h in range(N_HEADS):
            sl = slice(h * HEAD_DIM, (h + 1) * HEAD_DIM)
            acc_ref[:, sl] = acc_ref[:, sl] / l_ref[h]
        y = _dot((acc_ref[...] * sg_ref[...].astype(F32)).astype(BF16), wo_ref[...])
        xn_ref[...] = _layer_norm(alpha * x_ref[...] + y, lg_ref[...], lb_ref[...])


def _attend(q, sg, x, kb, vb, caches, bias, wo, lg, lb, *, tq, tk, adm_base, alpha):
    b, t, da = q.shape
    d = x.shape[-1]
    qt_tab, kt_tab = _pair_tables(t // tq, tq, tk, adm_base)
    qmap = lambda bi, p, qt, kt: (bi, qt[p], 0)
    kmap = lambda bi, p, qt, kt: (bi, kt[p], 0)
    const = lambda bi, p, qt, kt: (0, 0)
    tk_new = tk - (caches[0].shape[1] if caches else 0)
    cache_specs = [pl.BlockSpec((None,) + c.shape[1:], lambda bi, p, qt, kt: (bi, 0, 0)) for c in caches]
    kern = functools.partial(_attn_kernel, cached=bool(caches), tq=tq, tk=tk, adm_base=adm_base, alpha=alpha)
    return pl.pallas_call(
        kern,
        grid_spec=pltpu.PrefetchScalarGridSpec(
            num_scalar_prefetch=2,
            grid=(b, int(qt_tab.shape[0])),
            in_specs=[pl.BlockSpec((None, tq, da), qmap), pl.BlockSpec((None, tq, da), qmap),
                      pl.BlockSpec((None, tq, d), qmap),
                      pl.BlockSpec((None, tk_new, da), kmap), pl.BlockSpec((None, tk_new, da), kmap),
                      *cache_specs,
                      pl.BlockSpec((None, None, None, tk, bias.shape[-1]),
                                   lambda bi, p, qt, kt: (bi, qt[p], kt[p], 0, 0)),
                      pl.BlockSpec((da, d), const), pl.BlockSpec((1, d), const), pl.BlockSpec((1, d), const)],
            out_specs=pl.BlockSpec((None, tq, d), qmap),
            scratch_shapes=[pltpu.VMEM((N_HEADS, tq, LANES), F32), pltpu.VMEM((N_HEADS, tq, LANES), F32),
                            pltpu.VMEM((tq, da), F32), pltpu.VMEM((N_HEADS, tq, tk), F32),
                            pltpu.VMEM((N_HEADS, tq, tk), BF16), pltpu.VMEM((tq, tk), F32),
                            pltpu.VMEM((N_HEADS, tq, LANES), F32)]),
        out_shape=jax.ShapeDtypeStruct((b, t, d), F32),
        compiler_params=_cparams(("parallel", "arbitrary")),
        name="a_attend",
    )(qt_tab, kt_tab, q, sg, x, kb, vb, *caches, bias, wo, lg, lb)


def _attn_layer(x, rope, past, w_in, w_out, lg, lb, alpha, *, tm, tq, tk):
    b, t, d = x.shape
    da = N_HEADS * HEAD_DIM
    k32, v32, idx32, q, kb, vb, sg, qis, kib, wt = _aproj(x.reshape(b * t, d), rope, _split_a_weights(w_in), tm)
    r3 = lambda a: a.reshape(b, t, a.shape[-1])
    k32, v32, idx32, q, kb, vb, sg, qis, kib = map(r3, (k32, v32, idx32, q, kb, vb, sg, qis, kib))
    if past is None:
        adm_base, caches, kib_all = 0, (), kib
        total = t
    else:
        ck, cv, cki = past
        caches = (ck, cv)
        adm_base = ck.shape[1]
        total = adm_base + t
        pad = (-total) % LANES
        zpad = lambda w: jnp.zeros((b, pad, w), BF16)
        kb = jnp.concatenate([kb, zpad(da)], axis=1)
        vb = jnp.concatenate([vb, zpad(da)], axis=1)
        cki_pad = jnp.concatenate([cki.astype(BF16), jnp.zeros((b, adm_base, LANES - IDX_DIM), BF16)], axis=2)
        kib_all = jnp.concatenate([cki_pad, kib, zpad(LANES)], axis=1)
        tk = total + pad
    topk = min(TOPK_MAX, total // 4)
    tq_sel = tq
    if tq < LANES:
        assert t == tq
        tq_sel = LANES
        qis = jnp.concatenate([qis, jnp.zeros((b, tq_sel - t, qis.shape[-1]), BF16)], axis=1)
        wt = jnp.concatenate([wt.reshape(SUBLANES, b, t), jnp.zeros((SUBLANES, b, tq_sel - t), F32)],
                             axis=2).reshape(SUBLANES, b * tq_sel)
    bias = _select(qis, wt, kib_all, tq=tq_sel, tk=tk, adm_base=adm_base, topk=topk)
    xn = _attend(q, sg, x, kb, vb, caches, bias, w_out.astype(BF16), lg, lb,
                 tq=tq, tk=tk, adm_base=adm_base, alpha=alpha)
    return xn, k32.reshape(b, t, N_HEADS, HEAD_DIM), v32.reshape(b, t, N_HEADS, HEAD_DIM), idx32[:, :, :IDX_DIM]


def _conv_kernel(x_ref, prev_ref, wa_ref, wb_ref, wg_ref, cw_ref, cb_ref, ng_ref, nb_ref, wo_ref,
                 lg_ref, lb_ref, xn_ref, st_ref, ext_ref, c_ref, sh_ref, *, tm, alpha):
    t = pl.program_id(1)

    @pl.when(t == 0)
    def _first():
        ext_ref[0:CONV_HIST] = prev_ref[...]

    @pl.when(t > 0)
    def _carry():
        ext_ref[0:CONV_HIST] = ext_ref[tm:tm + CONV_HIST]

    x = x_ref[...]
    xb = x.astype(BF16)
    ext_ref[CONV_HIST:CONV_HIST + tm] = _dot(xb, wa_ref[...]) * jax.nn.sigmoid(_dot(xb, wb_ref[...]))
    st_ref[...] = ext_ref[tm:tm + CONV_HIST]

    off = CONV_HIST - (CONV_WIDTH - 1)
    rows = min(tm, CONV_ROWS)
    shifted = tm + CONV_HIST - SUBLANES
    for res in range(1, SUBLANES):
        sh_ref[res - 1] = ext_ref[res:res + shifted]
    for cs in range(x.shape[1] // LANES):
        sl = slice(cs * LANES, (cs + 1) * LANES)
        for r0 in range(0, tm, rows):
            acc = jnp.broadcast_to(cb_ref[:, sl], (rows, LANES))
            for w in range(CONV_WIDTH):
                res = (off + w) % SUBLANES
                a = r0 + (off + w) - res
                win = ext_ref[a:a + rows, sl] if res == 0 else sh_ref[res - 1, a:a + rows, sl]
                acc = acc + win * cw_ref[w:w + 1, sl]
            c_ref[r0:r0 + rows, sl] = acc

    cn = _silu(_layer_norm(c_ref[...], ng_ref[...], nb_ref[...]))
    y = _dot((cn * _silu(_dot(xb, wg_ref[...]))).astype(BF16), wo_ref[...])
    xn_ref[...] = _layer_norm(alpha * x + y, lg_ref[...], lb_ref[...])


def _conv_layer(x, prev, w_in, conv_w, conv_b, n_g, n_b, w_out, lg, lb, alpha, *, tm):
    b, t, d = x.shape
    dc = conv_w.shape[1]
    assert t >= CONV_HIST and t % tm == 0
    wa, wb, wg = (w_in[:, i * dc:(i + 1) * dc].astype(BF16) for i in range(3))
    prev = jnp.concatenate([jnp.zeros((b, CONV_HIST - (CONV_WIDTH - 1), dc), F32), prev], axis=1)
    cw = jnp.concatenate([conv_w, jnp.zeros((CONV_HIST - CONV_WIDTH, dc), F32)], axis=0)
    row = lambda bi, ti: (bi, ti, 0)
    per_b = lambda bi, ti: (bi, 0, 0)
    const = lambda bi, ti: (0, 0)
    vec = pl.BlockSpec((1, dc), const)
    xn, st = pl.pallas_call(
        functools.partial(_conv_kernel, tm=tm, alpha=alpha),
        grid=(b, t // tm),
        in_specs=[pl.BlockSpec((None, tm, d), row), pl.BlockSpec((None, CONV_HIST, dc), per_b),
                  pl.BlockSpec((d, dc), const), pl.BlockSpec((d, dc), const), pl.BlockSpec((d, dc), const),
                  pl.BlockSpec((CONV_HIST, dc), const), vec, vec, vec,
                  pl.BlockSpec((dc, d), const), pl.BlockSpec((1, d), const), pl.BlockSpec((1, d), const)],
        out_specs=[pl.BlockSpec((None, tm, d), row), pl.BlockSpec((None, CONV_HIST, dc), per_b)],
        out_shape=(jax.ShapeDtypeStruct((b, t, d), F32), jax.ShapeDtypeStruct((b, CONV_HIST, dc), F32)),
        scratch_shapes=[pltpu.VMEM((CONV_HIST + tm, dc), F32), pltpu.VMEM((tm, dc), F32),
                        pltpu.VMEM((SUBLANES - 1, tm + CONV_HIST - SUBLANES, dc), F32)],
        compiler_params=_cparams(("parallel", "arbitrary")),
        name="b_conv",
    )(x, prev, wa, wb, wg, cw, conv_b.reshape(1, dc), n_g.reshape(1, dc), n_b.reshape(1, dc),
      w_out.astype(BF16), lg, lb)
    return xn, st[:, CONV_HIST - (CONV_WIDTH - 1):]


def _pool_kernel(x_ref, prev_ref, wu_ref, wg_ref, wgrp_ref, sc_ref, wo_ref, lg_ref, lb_ref,
                 xn_ref, st_ref, ext_ref, mix_ref, *, tm, pos0, alpha):
    t = pl.program_id(1)

    @pl.when(t == 0)
    def _first():
        ext_ref[0:POOL_PAD] = prev_ref[...]

    @pl.when(t > 0)
    def _carry():
        ext_ref[0:POOL_PAD] = ext_ref[tm:tm + POOL_PAD]

    x = x_ref[...]
    xb = x.astype(BF16)
    ext_ref[POOL_PAD:POOL_PAD + tm] = _dot(xb, wu_ref[...])
    st_ref[...] = ext_ref[tm:tm + POOL_PAD]

    pos = pos0 + t * tm + lax.broadcasted_iota(I32, (tm, 1), 0)
    grp = x.shape[1] // len(POOL_WINDOWS)
    for gi, w in enumerate(POOL_WINDOWS):
        sl = slice(gi * grp, (gi + 1) * grp)
        u = ext_ref[POOL_PAD:POOL_PAD + tm, sl]
        ws = u
        for i in range(1, w):
            ws = ws + ext_ref[POOL_PAD - i:POOL_PAD - i + tm, sl]
        cnt = jnp.minimum(pos + 1, w).astype(F32)
        dlt = ws / cnt - u
        mix_ref[:, sl] = _dot(dlt.astype(BF16), wgrp_ref[gi]) * sc_ref[:, sl]

    y = _dot((mix_ref[...] * _silu(_dot(xb, wg_ref[...]))).astype(BF16), wo_ref[...])
    xn_ref[...] = _layer_norm(alpha * x + y, lg_ref[...], lb_ref[...])


def _pool_layer(x, prev, pos0, w_in, w_grp, scale, w_out, lg, lb, alpha, *, tm):
    b, t, d = x.shape
    dp = scale.shape[0]
    grp = dp // len(POOL_WINDOWS)
    assert t >= POOL_PAD and t % tm == 0
    wu, wg = (w_in[:, i * dp:(i + 1) * dp].astype(BF16) for i in range(2))
    prev = jnp.concatenate([jnp.zeros((b, POOL_PAD - POOL_HIST, dp), F32), prev], axis=1)
    row = lambda bi, ti: (bi, ti, 0)
    per_b = lambda bi, ti: (bi, 0, 0)
    const = lambda bi, ti: (0, 0)
    xn, st = pl.pallas_call(
        functools.partial(_pool_kernel, tm=tm, pos0=pos0, alpha=alpha),
        grid=(b, t // tm),
        in_specs=[pl.BlockSpec((None, tm, d), row), pl.BlockSpec((None, POOL_PAD, dp), per_b),
                  pl.BlockSpec((d, dp), const), pl.BlockSpec((d, dp), const),
                  pl.BlockSpec((len(POOL_WINDOWS), grp, grp), lambda bi, ti: (0, 0, 0)),
                  pl.BlockSpec((1, dp), const), pl.BlockSpec((dp, d), const),
                  pl.BlockSpec((1, d), const), pl.BlockSpec((1, d), const)],
        out_specs=[pl.BlockSpec((None, tm, d), row), pl.BlockSpec((None, POOL_PAD, dp), per_b)],
        out_shape=(jax.ShapeDtypeStruct((b, t, d), F32), jax.ShapeDtypeStruct((b, POOL_PAD, dp), F32)),
        scratch_shapes=[pltpu.VMEM((POOL_PAD + tm, dp), F32), pltpu.VMEM((tm, dp), F32)],
        compiler_params=_cparams(("parallel", "arbitrary")),
        name="c_pool",
    )(x, prev, wu, wg, w_grp.astype(BF16), scale.reshape(1, dp), w_out.astype(BF16), lg, lb)
    return xn, st[:, POOL_PAD - POOL_HIST:]


def kernel(x_prompt, x_sample, cache_k, cache_v, cache_kidx, state_conv, state_pool, w_in_a, w_out_a, w_in_b, conv_w_b, conv_bias_b, norm_g_b, norm_b_b, w_out_b, w_in_c, w_grp_c, scale_c, w_out_c, ln_g, ln_b):
    depth = ln_g.shape[0]
    alpha = (2.0 * depth) ** 0.25
    xp, xs = x_prompt, x_sample
    bp, tp, d = xp.shape
    bs, ts, _ = xs.shape
    past = cache_k.shape[2]
    da = N_HEADS * HEAD_DIM

    tm_p = min(256, tp)
    tm_s = min(256, bs * ts)
    rope_p = _rope_tables(jnp.arange(tp))
    rope_s = _rope_tables(past + jnp.arange(bs * ts) % ts)

    kp, vp, kip, cvp, plp = [], [], [], [], []
    ksm, vsm, kism, cvs, pls = [], [], [], [], []
    for i in range(depth):
        m, j = i % 3, i // 3
        lg, lb = ln_g[i].reshape(1, d), ln_b[i].reshape(1, d)
        if m == 0:
            xp, k1, v1, ki1 = _attn_layer(xp, rope_p, None, w_in_a[j], w_out_a[j], lg, lb, alpha,
                                          tm=tm_p, tq=min(256, tp), tk=min(512, tp))
            caches = (cache_k[j].reshape(bs, past, da), cache_v[j].reshape(bs, past, da), cache_kidx[j])
            xs, k2, v2, ki2 = _attn_layer(xs, rope_s, caches, w_in_a[j], w_out_a[j], lg, lb, alpha,
                                          tm=tm_s, tq=ts, tk=None)
            kp.append(k1); vp.append(v1); kip.append(ki1)
            ksm.append(k2); vsm.append(v2); kism.append(ki2)
        elif m == 1:
            args = (w_in_b[j], conv_w_b[j], conv_bias_b[j], norm_g_b[j], norm_b_b[j], w_out_b[j], lg, lb, alpha)
            xp, c1 = _conv_layer(xp, jnp.zeros((bp, CONV_WIDTH - 1, d), F32), *args, tm=min(256, tp))
            xs, c2 = _conv_layer(xs, state_conv[j], *args, tm=min(256, ts))
            cvp.append(c1); cvs.append(c2)
        else:
            args = (w_in_c[j], w_grp_c[j], scale_c[j], w_out_c[j], lg, lb, alpha)
            xp, p1 = _pool_layer(xp, jnp.zeros((bp, POOL_HIST, d), F32), 0, *args, tm=min(256, tp))
            xs, p2 = _pool_layer(xs, state_pool[j], past, *args, tm=min(256, ts))
            plp.append(p1); pls.append(p2)
    return (xp, xs,
            jnp.stack(kp), jnp.stack(vp), jnp.stack(kip), jnp.stack(cvp), jnp.stack(plp),
            jnp.stack(ksm), jnp.stack(vsm), jnp.stack(kism), jnp.stack(cvs), jnp.stack(pls))
```

```python
import functools

import jax
import jax.numpy as jnp
import numpy as np
from jax import lax
from jax.experimental import pallas as pl
from jax.experimental.pallas import tpu as pltpu

F32, BF16, I32 = jnp.float32, jnp.bfloat16, jnp.int32

LANES = 128
SUBLANES = 8
CHUNK = 64
N_HEADS, HEAD_DIM = 8, 128
IDX_HEADS, IDX_DIM = 4, 64
TOPK_MAX = 256
ROPE_THETA = 500000.0
CONV_WIDTH = 31
CONV_HIST = 32
POOL_WINDOWS = (2, 4, 8, 16)
POOL_HIST = 15
POOL_PAD = 16
LN_EPS = 1e-5
NEG_BIG = -1e30
LOG2E = float(np.log2(np.e))
INT_MAX = 2 ** 31 - 1
INT_MIN = -(2 ** 31)
ALL_TIES = 2 ** 24
NINF_KEY = int(np.int32(np.uint32(0xFF800000) ^ np.uint32(0x7FFFFFFF)))
BISECT_STEPS = 13
PROBE_OCTAVES = 3
MANTISSA_SPAN = 2 ** 23
SOFTMAX_ROWS = 32
CONV_ROWS = 128
QK_AHEAD = 2
VMEM_LIMIT = 56 * 1024 * 1024


def _cparams(sem):
    return pltpu.CompilerParams(dimension_semantics=sem, vmem_limit_bytes=VMEM_LIMIT)


def _dot(a, b):
    return jnp.dot(a, b, preferred_element_type=F32)


def _dot_nt(a, b):
    return lax.dot_general(a, b, (((1,), (1,)), ((), ())), preferred_element_type=F32)


def _layer_norm(z, g, b):
    mu = jnp.mean(z, axis=-1, keepdims=True)
    zc = z - mu
    var = jnp.mean(zc * zc, axis=-1, keepdims=True)
    return zc * lax.rsqrt(var + LN_EPS) * g + b


def _silu(x):
    return x * jax.nn.sigmoid(x)


def _rope(x, c, sa, sb, half):
    return x * c + pltpu.roll(x, LANES - half, 1) * sa + pltpu.roll(x, half, 1) * sb


def _rope_tables(pos):
    def tab(dim):
        r = dim // 4
        half = r // 2
        inv = ROPE_THETA ** (-jnp.arange(half, dtype=F32) * 2.0 / r)
        ang = pos.astype(F32)[:, None] * inv[None, :]
        cos, sin = jnp.cos(ang), jnp.sin(ang)
        n = pos.shape[0]
        c = jnp.concatenate([cos, cos, jnp.ones((n, LANES - r), F32)], axis=1)
        sa = jnp.concatenate([-sin, jnp.zeros((n, LANES - half), F32)], axis=1)
        sb = jnp.concatenate([jnp.zeros((n, half), F32), sin, jnp.zeros((n, LANES - r), F32)], axis=1)
        return [c, sa, sb]
    return jnp.stack(tab(HEAD_DIM) + tab(IDX_DIM), axis=0)


IDX_SLABS = IDX_HEADS + 1


def _aproj_kernel(x_ref, rope_ref, wq_ref, wk_ref, wv_ref, wg_ref, wi_ref, wwt_ref, k_alias, v_alias,
                  k32_ref, v32_ref, idx32_ref, q_ref, kb_ref, vb_ref, sg_ref, qis_ref, kib_ref, wt_ref):
    xb = x_ref[...].astype(BF16)
    wt_ref[...] = _dot_nt(wwt_ref[...], xb) * (IDX_HEADS ** -0.5)
    cq, saq, sbq = rope_ref[0], rope_ref[1], rope_ref[2]
    ci, sai, sbi = rope_ref[3], rope_ref[4], rope_ref[5]

    hq = _dot(xb, wq_ref[...])
    for h in range(N_HEADS):
        sl = slice(h * HEAD_DIM, (h + 1) * HEAD_DIM)
        q_ref[:, sl] = (_rope(hq[:, sl], cq, saq, sbq, HEAD_DIM // 8) * (HEAD_DIM ** -0.5 * LOG2E)).astype(BF16)
    hk = _dot(xb, wk_ref[...])
    heads = []
    for h in range(N_HEADS):
        sl = slice(h * HEAD_DIM, (h + 1) * HEAD_DIM)
        kh = _rope(hk[:, sl], cq, saq, sbq, HEAD_DIM // 8)
        heads.append(kh)
        kb_ref[:, sl] = kh.astype(BF16)
    k32_ref[...] = jnp.concatenate(heads, axis=1).reshape(k32_ref.shape)
    hv = _dot(xb, wv_ref[...])
    v32_ref[...] = hv.reshape(v32_ref.shape)
    vb_ref[...] = hv.astype(BF16)
    sg_ref[...] = _silu(_dot(xb, wg_ref[...])).astype(BF16)

    hi = _dot(xb, wi_ref[...])
    for h in range(IDX_HEADS):
        sl = slice(h * LANES, (h + 1) * LANES)
        qis_ref[:, sl] = (_rope(hi[:, sl], ci, sai, sbi, IDX_DIM // 8) * (IDX_DIM ** -0.5)).astype(BF16)
    last = _rope(hi[:, IDX_HEADS * LANES:], ci, sai, sbi, IDX_DIM // 8)
    idx32_ref[...] = last
    lane = lax.broadcasted_iota(I32, last.shape, 1)
    kib_ref[...] = jnp.where(lane < IDX_DIM, last, 0.0).astype(BF16)


def _split_a_weights(w_in):
    d = w_in.shape[0]
    da = N_HEADS * HEAD_DIM
    wq, wk, wv, wg = (w_in[:, i * da:(i + 1) * da].astype(BF16) for i in range(4))
    o = 4 * da
    slabs = []
    for h in range(IDX_HEADS):
        slabs += [w_in[:, o + h * IDX_DIM:o + (h + 1) * IDX_DIM], jnp.zeros((d, LANES - IDX_DIM), w_in.dtype)]
    o += IDX_HEADS * IDX_DIM
    slabs += [w_in[:, o:o + IDX_DIM + IDX_HEADS], jnp.zeros((d, LANES - IDX_DIM - IDX_HEADS), w_in.dtype)]
    wi = jnp.concatenate(slabs, axis=1).astype(BF16)
    wwt = jnp.concatenate([w_in[:, o + IDX_DIM:o + IDX_DIM + IDX_HEADS].T,
                           jnp.zeros((SUBLANES - IDX_HEADS, d), w_in.dtype)], axis=0).astype(BF16)
    return wq, wk, wv, wg, wi, wwt


def _aproj(x2, rope, weights, tm, kv_leaves, layer):
    n, d = x2.shape
    da = N_HEADS * HEAD_DIM
    rope_tiles = rope.shape[1] // tm
    row = lambda i: (i, 0)
    const = lambda i: (0, 0)
    leaf = pl.BlockSpec((None, tm, N_HEADS, HEAD_DIM), lambda i: (layer, i, 0, 0))
    wq, wk, wv, wg, wi, wwt = weights
    out_shape = (
        jax.ShapeDtypeStruct(kv_leaves[0].shape, F32), jax.ShapeDtypeStruct(kv_leaves[1].shape, F32),
        jax.ShapeDtypeStruct((n, LANES), F32),
        jax.ShapeDtypeStruct((n, da), BF16), jax.ShapeDtypeStruct((n, da), BF16),
        jax.ShapeDtypeStruct((n, da), BF16), jax.ShapeDtypeStruct((n, da), BF16),
        jax.ShapeDtypeStruct((n, IDX_HEADS * LANES), BF16), jax.ShapeDtypeStruct((n, LANES), BF16),
        jax.ShapeDtypeStruct((SUBLANES, n), F32),
    )
    return pl.pallas_call(
        _aproj_kernel,
        grid=(n // tm,),
        in_specs=[pl.BlockSpec((tm, d), row),
                  pl.BlockSpec((6, tm, LANES), lambda i: (0, i % rope_tiles, 0)),
                  pl.BlockSpec((d, da), const), pl.BlockSpec((d, da), const),
                  pl.BlockSpec((d, da), const), pl.BlockSpec((d, da), const),
                  pl.BlockSpec((d, IDX_SLABS * LANES), const), pl.BlockSpec((SUBLANES, d), const),
                  pl.BlockSpec(memory_space=pl.ANY), pl.BlockSpec(memory_space=pl.ANY)],
        out_specs=[leaf, leaf, pl.BlockSpec((tm, LANES), row),
                   pl.BlockSpec((tm, da), row), pl.BlockSpec((tm, da), row), pl.BlockSpec((tm, da), row),
                   pl.BlockSpec((tm, da), row), pl.BlockSpec((tm, IDX_HEADS * LANES), row),
                   pl.BlockSpec((tm, LANES), row), pl.BlockSpec((SUBLANES, tm), lambda i: (0, i))],
        out_shape=out_shape,
        input_output_aliases={8: 0, 9: 1},
        compiler_params=_cparams(("parallel",)),
        name="a_proj",
    )(x2, rope, wq, wk, wv, wg, wi, wwt, *kv_leaves)


def _index_keys(qis_ref, wt_ref, kib_ref, q_row0, k_row0, adm_base):
    tq = qis_ref.shape[0]
    tk = kib_ref.shape[0]
    kib = kib_ref[...]
    s = None
    for h in range(IDX_HEADS):
        d = _dot_nt(kib, qis_ref[:, h * LANES:(h + 1) * LANES])
        t = jnp.maximum(d, 0.0) * wt_ref[h:h + 1, :]
        s = t if s is None else s + t
    bits = lax.bitcast_convert_type(s, I32)
    key = bits ^ ((bits >> 31) & 0x7FFFFFFF)
    key = jnp.where(key == -1, 0, key)
    kidx = k_row0 + lax.broadcasted_iota(I32, (tk, 1), 0)
    qrow = q_row0 + lax.broadcasted_iota(I32, (1, tq), 1)
    n_adm = adm_base + (qrow // CHUNK + 1) * CHUNK
    return jnp.where(kidx < n_adm, key, NINF_KEY)


def _pair_tables(n_q, tq, tk, adm_base):
    qt, kt = [], []
    for q in range(n_q):
        last = (adm_base + ((q * tq + tq - 1) // CHUNK + 1) * CHUNK - 1) // tk
        for k in range(last + 1):
            qt.append(q)
            kt.append(k)
    return jnp.asarray(qt, I32), jnp.asarray(kt, I32)


def _last_key_tile(qt, tq, tk, adm_base):
    return (adm_base + ((qt * tq + tq - 1) // CHUNK + 1) * CHUNK - 1) // tk


def _select_kernel(qt_ref, kt_ref, qis_ref, wt_ref, kib_ref, bias_ref, keys_ref, tri_ref,
                   *, tq, tk, n_kt, adm_base, topk):
    p = pl.program_id(1)
    qt = qt_ref[p]
    kt = kt_ref[p]
    groups = tq // LANES
    slabs = tk // SUBLANES

    @pl.when(p == 0)
    def _tri():
        r = lax.broadcasted_iota(I32, tri_ref.shape, 0)
        c = lax.broadcasted_iota(I32, tri_ref.shape, 1)
        tri_ref[...] = jnp.where(c <= r, 1.0, 0.0).astype(BF16)

    keys_ref[kt] = _index_keys(qis_ref, wt_ref, kib_ref, qt * tq, kt * tk, adm_base)

    @pl.when(kt == _last_key_tile(qt, tq, tk, adm_base))
    def _search():
        n_tiles = kt + 1

        def fold_keys(cand, step, init, combine):
            cb = [jnp.broadcast_to(cand[:, g * LANES:(g + 1) * LANES], (SUBLANES, LANES)) for g in range(groups)]

            def body(t, accs):
                accs = list(accs)
                for r in range(slabs):
                    for g in range(groups):
                        v = keys_ref[t, r * SUBLANES:(r + 1) * SUBLANES, g * LANES:(g + 1) * LANES]
                        j = 2 * g + r % 2
                        accs[j] = step(accs[j], v, cb[g])
                return tuple(accs)
            start = jnp.full((SUBLANES, LANES), init, I32)
            accs = lax.fori_loop(0, n_tiles, body, (start,) * (2 * groups))
            return [combine(accs[2 * g], accs[2 * g + 1]) for g in range(groups)]

        def count_ge(cand):
            tot = fold_keys(cand, lambda acc, v, c: acc + jnp.where(v >= c, 1, 0), 0,
                            lambda a, b: jnp.sum((a + b).astype(F32), axis=0, keepdims=True))
            return jnp.concatenate(tot, axis=1).astype(I32)

        def key_max():
            tot = fold_keys(jnp.zeros((1, tq), I32), lambda acc, v, c: jnp.maximum(acc, v), INT_MIN,
                            lambda a, b: jnp.max(jnp.maximum(a, b), axis=0, keepdims=True))
            return jnp.concatenate(tot, axis=1)

        def max_below(cand):
            tot = fold_keys(cand, lambda acc, v, c: jnp.maximum(acc, jnp.where(v < c, v, INT_MIN)), INT_MIN,
                            lambda a, b: jnp.max(jnp.maximum(a, b), axis=0, keepdims=True))
            return jnp.concatenate(tot, axis=1)

        qrow = qt * tq + lax.broadcasted_iota(I32, (1, tq), 1)
        n_adm = adm_base + (qrow // CHUNK + 1) * CHUNK
        short = n_adm < topk
        full = lambda v: jnp.full((1, tq), v, I32)
        c_zero = count_ge(full(0))
        c_pos = count_ge(full(1))
        pos = c_pos >= topk
        neg = c_zero < topk
        lo = jnp.where(pos, 1, jnp.where(neg, NINF_KEY + 1, 0))
        hi = jnp.where(pos, INT_MAX, jnp.where(neg, 0, 1))
        c_lo = jnp.where(pos, c_pos, jnp.where(neg, n_adm, c_zero))
        c_hi = jnp.where(pos, 0, jnp.where(neg, c_zero, c_pos))
        top = key_max()
        probe = jnp.maximum(top - PROBE_OCTAVES * MANTISSA_SPAN, 1)
        c_probe = count_ge(probe)
        above = pos & (c_probe >= topk)
        below = pos & (c_probe < topk)
        lo = jnp.where(above, probe, lo)
        c_lo = jnp.where(above, c_probe, c_lo)
        hi = jnp.where(above, top + 1, jnp.where(below, probe, hi))
        c_hi = jnp.where(below, c_probe, c_hi)

        def active(lo, hi, c_lo, c_hi):
            return jnp.logical_not(short) & (c_lo != topk) & (c_hi != topk - 1) & (hi - lo > 1)

        def unfinished(state):
            return jnp.max(active(*state).astype(I32)) > 0

        def bisect(state):
            lo, hi, c_lo, c_hi = state
            act = active(lo, hi, c_lo, c_hi)
            cand = jnp.where(act, lo + ((hi - lo) >> 1), lo)
            c = count_ge(cand)
            up = act & (c >= topk)
            down = act & (c < topk)
            return (jnp.where(up, cand, lo), jnp.where(down, cand, hi),
                    jnp.where(up, c, c_lo), jnp.where(down, c, c_hi))
        state = lax.fori_loop(0, BISECT_STEPS, lambda _, s: bisect(s), (lo, hi, c_lo, c_hi))
        lo, hi, c_lo, c_hi = lax.while_loop(unfinished, bisect, state)

        exact = c_lo == topk
        thr = jnp.where(short, NINF_KEY, jnp.where(exact, lo, max_below(hi)))
        need = jnp.where(short, 0, jnp.where(exact, ALL_TIES, topk - c_hi)).astype(F32)

        def emit(t, run):
            key = keys_ref[t]
            eq = key == thr
            pref = _dot(tri_ref[...], jnp.where(eq, 1.0, 0.0).astype(BF16))
            sel = (key > thr) | (eq & (run + pref <= need))
            bias_ref[t] = jnp.where(sel, 0.0, NEG_BIG).astype(BF16)
            return run + pref[tk - 1:tk, :]
        lax.fori_loop(0, n_tiles, emit, jnp.zeros((1, tq), F32))

        def fill(t, carry):
            bias_ref[t] = jnp.full((tk, tq), NEG_BIG, BF16)
            return carry
        lax.fori_loop(n_tiles, n_kt, fill, 0)


def _select(qis, wt, kib, *, tq, tk, adm_base, topk):
    b, t, _ = qis.shape
    n_kt = kib.shape[1] // tk
    n_q = t // tq
    qt_tab, kt_tab = _pair_tables(n_q, tq, tk, adm_base)
    kern = functools.partial(_select_kernel, tq=tq, tk=tk, n_kt=n_kt, adm_base=adm_base, topk=topk)
    return pl.pallas_call(
        kern,
        grid_spec=pltpu.PrefetchScalarGridSpec(
            num_scalar_prefetch=2,
            grid=(b, int(qt_tab.shape[0])),
            in_specs=[pl.BlockSpec((None, tq, IDX_HEADS * LANES), lambda bi, p, qt, kt: (bi, qt[p], 0)),
                      pl.BlockSpec((SUBLANES, tq), lambda bi, p, qt, kt: (0, bi * n_q + qt[p])),
                      pl.BlockSpec((None, tk, LANES), lambda bi, p, qt, kt: (bi, kt[p], 0))],
            out_specs=pl.BlockSpec((None, None, n_kt, tk, tq), lambda bi, p, qt, kt: (bi, qt[p], 0, 0, 0)),
            scratch_shapes=[pltpu.VMEM((n_kt, tk, tq), I32), pltpu.VMEM((tk, tk), BF16)]),
        out_shape=jax.ShapeDtypeStruct((b, n_q, n_kt, tk, tq), BF16),
        compiler_params=_cparams(("parallel", "arbitrary")),
        name="a_select",
    )(qt_tab, kt_tab, qis, wt, kib)


def _attn_kernel(qt_ref, kt_ref, q_ref, sg_ref, x_ref, k_ref, v_ref, *rest, cached, tq, tk, adm_base, alpha):
    kc_ref, vc_ref = rest[:2] if cached else (None, None)
    (bias_ref, wo_ref, lg_ref, lb_ref, xn_ref,
     m_ref, l_ref, acc_ref, s_ref, p_ref, b32_ref, a_ref) = rest[2:] if cached else rest
    p = pl.program_id(1)
    qt = qt_ref[p]
    kt = kt_ref[p]
    cols = tk // LANES
    rb = min(tq, SOFTMAX_ROWS)

    def keys_of(h, new_ref, cache_ref):
        sl = slice(h * HEAD_DIM, (h + 1) * HEAD_DIM)
        if cache_ref is None:
            return new_ref[:, sl]
        return jnp.concatenate([cache_ref[:, sl].astype(BF16), new_ref[:, sl]], axis=0)

    @pl.when(kt == 0)
    def _init():
        m_ref[...] = jnp.full(m_ref.shape, NEG_BIG, F32)
        l_ref[...] = jnp.zeros(l_ref.shape, F32)
        acc_ref[...] = jnp.zeros(acc_ref.shape, F32)

    b32_ref[...] = bias_ref[...].astype(F32).T[:tq]

    def lane_blocks(a):
        return [a[:, c * LANES:(c + 1) * LANES] for c in range(cols)]

    def head(h):
        return slice(h * HEAD_DIM, (h + 1) * HEAD_DIM)

    for h in range(min(QK_AHEAD, N_HEADS)):
        s_ref[h] = _dot_nt(q_ref[:, head(h)], keys_of(h, k_ref, kc_ref))
    for h in range(N_HEADS):
        sl = head(h)
        if h + QK_AHEAD < N_HEADS:
            s_ref[h + QK_AHEAD] = _dot_nt(q_ref[:, head(h + QK_AHEAD)], keys_of(h + QK_AHEAD, k_ref, kc_ref))
        for r0 in range(0, tq, rb):
            rows = slice(r0, r0 + rb)
            blocks = lane_blocks(s_ref[h, rows, :] + b32_ref[rows, :])
            m_prev = m_ref[h, rows]
            m_new = jnp.maximum(m_prev, jnp.max(functools.reduce(jnp.maximum, blocks), axis=1, keepdims=True))
            a_ref[h, rows] = jnp.exp2(m_prev - m_new)
            m_ref[h, rows] = m_new
        for r0 in range(0, tq, rb):
            rows = slice(r0, r0 + rb)
            m_r = m_ref[h, rows]
            blocks = [jnp.exp2(blk - m_r) for blk in lane_blocks(s_ref[h, rows, :] + b32_ref[rows, :])]
            p_ref[h, rows, :] = jnp.concatenate(blocks, axis=1).astype(BF16)
            l_ref[h, rows] = (a_ref[h, rows] * l_ref[h, rows]
                              + jnp.sum(functools.reduce(jnp.add, blocks), axis=1, keepdims=True))
        acc_ref[:, sl] = a_ref[h] * acc_ref[:, sl] + _dot(p_ref[h], keys_of(h, v_ref, vc_ref))

    @pl.when(kt == _last_key_tile(qt, tq, tk, adm_base))
    def _finish():
        for h in range(N_HEADS):
            sl = slice(h * HEAD_DIM, (h + 1) * HEAD_DIM)
            acc_ref[:, sl] = acc_ref[:, sl] / l_ref[h]
        y = _dot((acc_ref[...] * sg_ref[...].astype(F32)).astype(BF16), wo_ref[...])
        xn_ref[...] = _layer_norm(alpha * x_ref[...] + y, lg_ref[...], lb_ref[...])


def _attend(q, sg, x, kb, vb, caches, bias, wo, lg, lb, *, tq, tk, adm_base, alpha):
    b, t, da = q.shape
    d = x.shape[-1]
    qt_tab, kt_tab = _pair_tables(t // tq, tq, tk, adm_base)
    qmap = lambda bi, p, qt, kt: (bi, qt[p], 0)
    kmap = lambda bi, p, qt, kt: (bi, kt[p], 0)
    const = lambda bi, p, qt, kt: (0, 0)
    tk_new = tk - (caches[0].shape[1] if caches else 0)
    cache_specs = [pl.BlockSpec((None,) + c.shape[1:], lambda bi, p, qt, kt: (bi, 0, 0)) for c in caches]
    kern = functools.partial(_attn_kernel, cached=bool(caches), tq=tq, tk=tk, adm_base=adm_base, alpha=alpha)
    return pl.pallas_call(
        kern,
        grid_spec=pltpu.PrefetchScalarGridSpec(
            num_scalar_prefetch=2,
            grid=(b, int(qt_tab.shape[0])),
            in_specs=[pl.BlockSpec((None, tq, da), qmap), pl.BlockSpec((None, tq, da), qmap),
                      pl.BlockSpec((None, tq, d), qmap),
                      pl.BlockSpec((None, tk_new, da), kmap), pl.BlockSpec((None, tk_new, da), kmap),
                      *cache_specs,
                      pl.BlockSpec((None, None, None, tk, bias.shape[-1]),
                                   lambda bi, p, qt, kt: (bi, qt[p], kt[p], 0, 0)),
                      pl.BlockSpec((da, d), const), pl.BlockSpec((1, d), const), pl.BlockSpec((1, d), const)],
            out_specs=pl.BlockSpec((None, tq, d), qmap),
            scratch_shapes=[pltpu.VMEM((N_HEADS, tq, LANES), F32), pltpu.VMEM((N_HEADS, tq, LANES), F32),
                            pltpu.VMEM((tq, da), F32), pltpu.VMEM((N_HEADS, tq, tk), F32),
                            pltpu.VMEM((N_HEADS, tq, tk), BF16), pltpu.VMEM((tq, tk), F32),
                            pltpu.VMEM((N_HEADS, tq, LANES), F32)]),
        out_shape=jax.ShapeDtypeStruct((b, t, d), F32),
        compiler_params=_cparams(("parallel", "arbitrary")),
        name="a_attend",
    )(qt_tab, kt_tab, q, sg, x, kb, vb, *caches, bias, wo, lg, lb)


def _attn_layer(x, rope, past, w_in, w_out, lg, lb, alpha, kv_leaves, layer, *, tm, tq, tk):
    b, t, d = x.shape
    da = N_HEADS * HEAD_DIM
    k_leaf, v_leaf, idx32, q, kb, vb, sg, qis, kib, wt = _aproj(
        x.reshape(b * t, d), rope, _split_a_weights(w_in), tm, kv_leaves, layer)
    r3 = lambda a: a.reshape(b, t, a.shape[-1])
    idx32, q, kb, vb, sg, qis, kib = map(r3, (idx32, q, kb, vb, sg, qis, kib))
    if past is None:
        adm_base, caches, kib_all = 0, (), kib
        total = t
    else:
        ck, cv, cki = past
        caches = (ck, cv)
        adm_base = ck.shape[1]
        total = adm_base + t
        pad = (-total) % LANES
        zpad = lambda w: jnp.zeros((b, pad, w), BF16)
        kb = jnp.concatenate([kb, zpad(da)], axis=1)
        vb = jnp.concatenate([vb, zpad(da)], axis=1)
        cki_pad = jnp.concatenate([cki.astype(BF16), jnp.zeros((b, adm_base, LANES - IDX_DIM), BF16)], axis=2)
        kib_all = jnp.concatenate([cki_pad, kib, zpad(LANES)], axis=1)
        tk = total + pad
    topk = min(TOPK_MAX, total // 4)
    tq_sel = tq
    if tq < LANES:
        assert t == tq
        tq_sel = LANES
        qis = jnp.concatenate([qis, jnp.zeros((b, tq_sel - t, qis.shape[-1]), BF16)], axis=1)
        wt = jnp.concatenate([wt.reshape(SUBLANES, b, t), jnp.zeros((SUBLANES, b, tq_sel - t), F32)],
                             axis=2).reshape(SUBLANES, b * tq_sel)
    bias = _select(qis, wt, kib_all, tq=tq_sel, tk=tk, adm_base=adm_base, topk=topk)
    xn = _attend(q, sg, x, kb, vb, caches, bias, w_out.astype(BF16), lg, lb,
                 tq=tq, tk=tk, adm_base=adm_base, alpha=alpha)
    return xn, (k_leaf, v_leaf), idx32[:, :, :IDX_DIM]


def _conv_kernel(x_ref, prev_ref, wa_ref, wb_ref, wg_ref, cw_ref, cb_ref, ng_ref, nb_ref, wo_ref,
                 lg_ref, lb_ref, xn_ref, st_ref, ext_ref, c_ref, sh_ref, *, tm, alpha):
    t = pl.program_id(1)

    @pl.when(t == 0)
    def _first():
        ext_ref[0:CONV_HIST] = prev_ref[...]

    @pl.when(t > 0)
    def _carry():
        ext_ref[0:CONV_HIST] = ext_ref[tm:tm + CONV_HIST]

    x = x_ref[...]
    xb = x.astype(BF16)
    ext_ref[CONV_HIST:CONV_HIST + tm] = _dot(xb, wa_ref[...]) * jax.nn.sigmoid(_dot(xb, wb_ref[...]))
    st_ref[...] = ext_ref[tm:tm + CONV_HIST]

    off = CONV_HIST - (CONV_WIDTH - 1)
    rows = min(tm, CONV_ROWS)
    shifted = tm + CONV_HIST - SUBLANES
    for res in range(1, SUBLANES):
        sh_ref[res - 1] = ext_ref[res:res + shifted]
    for cs in range(x.shape[1] // LANES):
        sl = slice(cs * LANES, (cs + 1) * LANES)
        for r0 in range(0, tm, rows):
            acc = jnp.broadcast_to(cb_ref[:, sl], (rows, LANES))
            for w in range(CONV_WIDTH):
                res = (off + w) % SUBLANES
                a = r0 + (off + w) - res
                win = ext_ref[a:a + rows, sl] if res == 0 else sh_ref[res - 1, a:a + rows, sl]
                acc = acc + win * cw_ref[w:w + 1, sl]
            c_ref[r0:r0 + rows, sl] = acc

    cn = _silu(_layer_norm(c_ref[...], ng_ref[...], nb_ref[...]))
    y = _dot((cn * _silu(_dot(xb, wg_ref[...]))).astype(BF16), wo_ref[...])
    xn_ref[...] = _layer_norm(alpha * x + y, lg_ref[...], lb_ref[...])


def _conv_layer(x, prev, w_in, conv_w, conv_b, n_g, n_b, w_out, lg, lb, alpha, *, tm):
    b, t, d = x.shape
    dc = conv_w.shape[1]
    assert t >= CONV_HIST and t % tm == 0
    wa, wb, wg = (w_in[:, i * dc:(i + 1) * dc].astype(BF16) for i in range(3))
    prev = jnp.concatenate([jnp.zeros((b, CONV_HIST - (CONV_WIDTH - 1), dc), F32), prev], axis=1)
    cw = jnp.concatenate([conv_w, jnp.zeros((CONV_HIST - CONV_WIDTH, dc), F32)], axis=0)
    row = lambda bi, ti: (bi, ti, 0)
    per_b = lambda bi, ti: (bi, 0, 0)
    const = lambda bi, ti: (0, 0)
    vec = pl.BlockSpec((1, dc), const)
    xn, st = pl.pallas_call(
        functools.partial(_conv_kernel, tm=tm, alpha=alpha),
        grid=(b, t // tm),
        in_specs=[pl.BlockSpec((None, tm, d), row), pl.BlockSpec((None, CONV_HIST, dc), per_b),
                  pl.BlockSpec((d, dc), const), pl.BlockSpec((d, dc), const), pl.BlockSpec((d, dc), const),
                  pl.BlockSpec((CONV_HIST, dc), const), vec, vec, vec,
                  pl.BlockSpec((dc, d), const), pl.BlockSpec((1, d), const), pl.BlockSpec((1, d), const)],
        out_specs=[pl.BlockSpec((None, tm, d), row), pl.BlockSpec((None, CONV_HIST, dc), per_b)],
        out_shape=(jax.ShapeDtypeStruct((b, t, d), F32), jax.ShapeDtypeStruct((b, CONV_HIST, dc), F32)),
        scratch_shapes=[pltpu.VMEM((CONV_HIST + tm, dc), F32), pltpu.VMEM((tm, dc), F32),
                        pltpu.VMEM((SUBLANES - 1, tm + CONV_HIST - SUBLANES, dc), F32)],
        compiler_params=_cparams(("parallel", "arbitrary")),
        name="b_conv",
    )(x, prev, wa, wb, wg, cw, conv_b.reshape(1, dc), n_g.reshape(1, dc), n_b.reshape(1, dc),
      w_out.astype(BF16), lg, lb)
    return xn, st[:, CONV_HIST - (CONV_WIDTH - 1):]


def _pool_kernel(x_ref, prev_ref, wu_ref, wg_ref, wgrp_ref, sc_ref, wo_ref, lg_ref, lb_ref,
                 xn_ref, st_ref, ext_ref, mix_ref, *, tm, pos0, alpha):
    t = pl.program_id(1)

    @pl.when(t == 0)
    def _first():
        ext_ref[0:POOL_PAD] = prev_ref[...]

    @pl.when(t > 0)
    def _carry():
        ext_ref[0:POOL_PAD] = ext_ref[tm:tm + POOL_PAD]

    x = x_ref[...]
    xb = x.astype(BF16)
    ext_ref[POOL_PAD:POOL_PAD + tm] = _dot(xb, wu_ref[...])
    st_ref[...] = ext_ref[tm:tm + POOL_PAD]

    pos = pos0 + t * tm + lax.broadcasted_iota(I32, (tm, 1), 0)
    grp = x.shape[1] // len(POOL_WINDOWS)
    for gi, w in enumerate(POOL_WINDOWS):
        sl = slice(gi * grp, (gi + 1) * grp)
        u = ext_ref[POOL_PAD:POOL_PAD + tm, sl]
        ws = u
        for i in range(1, w):
            ws = ws + ext_ref[POOL_PAD - i:POOL_PAD - i + tm, sl]
        cnt = jnp.minimum(pos + 1, w).astype(F32)
        dlt = ws / cnt - u
        mix_ref[:, sl] = _dot(dlt.astype(BF16), wgrp_ref[gi]) * sc_ref[:, sl]

    y = _dot((mix_ref[...] * _silu(_dot(xb, wg_ref[...]))).astype(BF16), wo_ref[...])
    xn_ref[...] = _layer_norm(alpha * x + y, lg_ref[...], lb_ref[...])


def _pool_layer(x, prev, pos0, w_in, w_grp, scale, w_out, lg, lb, alpha, *, tm):
    b, t, d = x.shape
    dp = scale.shape[0]
    grp = dp // len(POOL_WINDOWS)
    assert t >= POOL_PAD and t % tm == 0
    wu, wg = (w_in[:, i * dp:(i + 1) * dp].astype(BF16) for i in range(2))
    prev = jnp.concatenate([jnp.zeros((b, POOL_PAD - POOL_HIST, dp), F32), prev], axis=1)
    row = lambda bi, ti: (bi, ti, 0)
    per_b = lambda bi, ti: (bi, 0, 0)
    const = lambda bi, ti: (0, 0)
    xn, st = pl.pallas_call(
        functools.partial(_pool_kernel, tm=tm, pos0=pos0, alpha=alpha),
        grid=(b, t // tm),
        in_specs=[pl.BlockSpec((None, tm, d), row), pl.BlockSpec((None, POOL_PAD, dp), per_b),
                  pl.BlockSpec((d, dp), const), pl.BlockSpec((d, dp), const),
                  pl.BlockSpec((len(POOL_WINDOWS), grp, grp), lambda bi, ti: (0, 0, 0)),
                  pl.BlockSpec((1, dp), const), pl.BlockSpec((dp, d), const),
                  pl.BlockSpec((1, d), const), pl.BlockSpec((1, d), const)],
        out_specs=[pl.BlockSpec((None, tm, d), row), pl.BlockSpec((None, POOL_PAD, dp), per_b)],
        out_shape=(jax.ShapeDtypeStruct((b, t, d), F32), jax.ShapeDtypeStruct((b, POOL_PAD, dp), F32)),
        scratch_shapes=[pltpu.VMEM((POOL_PAD + tm, dp), F32), pltpu.VMEM((tm, dp), F32)],
        compiler_params=_cparams(("parallel", "arbitrary")),
        name="c_pool",
    )(x, prev, wu, wg, w_grp.astype(BF16), scale.reshape(1, dp), w_out.astype(BF16), lg, lb)
    return xn, st[:, POOL_PAD - POOL_HIST:]


def kernel(x_prompt, x_sample, cache_k, cache_v, cache_kidx, state_conv, state_pool, w_in_a, w_out_a, w_in_b, conv_w_b, conv_bias_b, norm_g_b, norm_b_b, w_out_b, w_in_c, w_grp_c, scale_c, w_out_c, ln_g, ln_b):
    depth = ln_g.shape[0]
    alpha = (2.0 * depth) ** 0.25
    xp, xs = x_prompt, x_sample
    bp, tp, d = xp.shape
    bs, ts, _ = xs.shape
    past = cache_k.shape[2]
    da = N_HEADS * HEAD_DIM

    tm_p = min(256, tp)
    tm_s = min(256, bs * ts)
    rope_p = _rope_tables(jnp.arange(tp))
    rope_s = _rope_tables(past + jnp.arange(bs * ts) % ts)

    n_a = w_in_a.shape[0]
    kv_p = tuple(jnp.zeros((n_a, bp * tp, N_HEADS, HEAD_DIM), F32) for _ in range(2))
    kv_s = tuple(jnp.zeros((n_a, bs * ts, N_HEADS, HEAD_DIM), F32) for _ in range(2))
    kip, cvp, plp = [], [], []
    kism, cvs, pls = [], [], []
    for i in range(depth):
        m, j = i % 3, i // 3
        lg, lb = ln_g[i].reshape(1, d), ln_b[i].reshape(1, d)
        if m == 0:
            xp, kv_p, ki1 = _attn_layer(xp, rope_p, None, w_in_a[j], w_out_a[j], lg, lb, alpha, kv_p, j,
                                        tm=tm_p, tq=min(256, tp), tk=min(512, tp))
            caches = (cache_k[j].reshape(bs, past, da), cache_v[j].reshape(bs, past, da), cache_kidx[j])
            xs, kv_s, ki2 = _attn_layer(xs, rope_s, caches, w_in_a[j], w_out_a[j], lg, lb, alpha, kv_s, j,
                                        tm=tm_s, tq=ts, tk=None)
            kip.append(ki1)
            kism.append(ki2)
        elif m == 1:
            args = (w_in_b[j], conv_w_b[j], conv_bias_b[j], norm_g_b[j], norm_b_b[j], w_out_b[j], lg, lb, alpha)
            xp, c1 = _conv_layer(xp, jnp.zeros((bp, CONV_WIDTH - 1, d), F32), *args, tm=min(256, tp))
            xs, c2 = _conv_layer(xs, state_conv[j], *args, tm=min(256, ts))
            cvp.append(c1); cvs.append(c2)
        else:
            args = (w_in_c[j], w_grp_c[j], scale_c[j], w_out_c[j], lg, lb, alpha)
            xp, p1 = _pool_layer(xp, jnp.zeros((bp, POOL_HIST, d), F32), 0, *args, tm=min(256, tp))
            xs, p2 = _pool_layer(xs, state_pool[j], past, *args, tm=min(256, ts))
            plp.append(p1); pls.append(p2)
    leaf_p = lambda a: a.reshape(n_a, bp, tp, N_HEADS, HEAD_DIM)
    leaf_s = lambda a: a.reshape(n_a, bs, ts, N_HEADS, HEAD_DIM)
    return (xp, xs,
            leaf_p(kv_p[0]), leaf_p(kv_p[1]), jnp.stack(kip), jnp.stack(cvp), jnp.stack(plp),
            leaf_s(kv_s[0]), leaf_s(kv_s[1]), jnp.stack(kism), jnp.stack(cvs), jnp.stack(pls))
```
